```python
import jax, jax.numpy as jnp
from jax import lax
import numpy as np

D_MODEL = 1024
BATCH = 8
SEQ = 2048
DEPTH = 1

GRID_W = 64
MEM_LEN = 256
NA_HEADS = 8
NA_HEAD_DIM = 64
NA_WIDTH = NA_HEADS * NA_HEAD_DIM
NA_KH_MAX = 8
NA_KW = 16
NA_QB = 16
NA_KB = 32
HG_HEADS = 4
HG_KDIM = 128
HG_VDIM = 128
HG_QK = HG_HEADS * HG_KDIM
HG_V = HG_HEADS * HG_VDIM
HG_CHUNK = 64
MEM_HEADS = 4
MEM_HEAD_DIM = 128
MEM_WIDTH = MEM_HEADS * MEM_HEAD_DIM
N_BRANCH = 3
D_FF = 4 * D_MODEL
ALPHA = (2.0 * DEPTH) ** 0.25
BETA = (8.0 * DEPTH) ** -0.25
LN_EPS = 1e-5
RMS_EPS = 1e-6

_IN_SIZES = (NA_WIDTH, NA_WIDTH, NA_WIDTH, HG_QK, HG_V, HG_V, HG_QK, HG_QK, MEM_WIDTH, N_BRANCH * D_MODEL)
_IN_COL_SCALE = (1.0, 1.0, BETA, 1.0, BETA, 1.0, 1.0, 1.0, 1.0, 1.0)
IN_COLS = sum(_IN_SIZES)

kernel_name = "hybrid_na_hgrn2_memattn_encoder"


def _split_points():
    return np.cumsum(_IN_SIZES)[:-1].tolist()


def layer_norm(x, g, b):
    xf = x.astype(jnp.float32)
    mu = jnp.mean(xf, axis=-1, keepdims=True)
    var = jnp.mean(jnp.square(xf - mu), axis=-1, keepdims=True)
    return ((xf - mu) * lax.rsqrt(var + LN_EPS) * g + b).astype(x.dtype)


def neighbourhood_attention(q, k, v, rpb):
    B, S, H, dh = q.shape
    rows = S // GRID_W
    kh = min(NA_KH_MAX, rows)
    ncb = GRID_W // NA_QB
    to_grid = lambda t: t.reshape(B, rows, GRID_W, H, dh).transpose(0, 3, 1, 2, 4)
    qg, kg, vg = to_grid(q), to_grid(k), to_grid(v)

    row_start = jnp.clip(jnp.arange(rows) - kh // 2, 0, rows - kh)
    col_blk = jnp.clip(jnp.arange(ncb) * NA_QB - NA_KW // 2, 0, GRID_W - NA_KB)
    kcol = col_blk[:, None] + jnp.arange(NA_KB)
    qcol = jnp.arange(GRID_W).reshape(ncb, NA_QB)
    c0 = jnp.clip(qcol - NA_KW // 2, 0, GRID_W - NA_KW)
    kc = kcol[:, None, :]
    valid = (kc >= c0[..., None]) & (kc < c0[..., None] + NA_KW)
    dc_idx = jnp.clip(kc - qcol[..., None] + NA_KW - 1, 0, 2 * NA_KW - 2)
    rpb_c = rpb.astype(jnp.float32)[:, :, dc_idx]
    scale = dh ** -0.5

    def one_row(r):
        rs = row_start[r]
        k_rows = lax.dynamic_slice_in_dim(kg, rs, kh, axis=2)
        v_rows = lax.dynamic_slice_in_dim(vg, rs, kh, axis=2)
        k_blk = k_rows[:, :, :, kcol]
        v_blk = v_rows[:, :, :, kcol]
        q_row = lax.dynamic_index_in_dim(qg, r, axis=2, keepdims=False).reshape(B, H, ncb, NA_QB, dh)
        s = jnp.einsum('bhnqd,bhanjd->bhnqaj', q_row, k_blk).astype(jnp.float32) * scale
        dr_idx = rs + jnp.arange(kh) - r + NA_KH_MAX - 1
        bias = jnp.take(rpb_c, dr_idx, axis=1).transpose(0, 2, 3, 1, 4)
        s = jnp.where(valid[:, :, None, :], s + bias, -jnp.inf)
        p = jax.nn.softmax(s.reshape(B, H, ncb, NA_QB, kh * NA_KB), axis=-1).reshape(s.shape)
        o = jnp.einsum('bhnqaj,bhanjd->bhnqd', p.astype(v.dtype), v_blk)
        return o.reshape(B, H, GRID_W, dh)

    out = lax.map(one_row, jnp.arange(rows))
    return out.transpose(1, 0, 3, 2, 4).reshape(B, S, H * dh)


def hgrn2_scan(q, k, v, g):
    B, S, H, K = q.shape
    V = v.shape[-1]
    nc = S // HG_CHUNK
    f32 = jnp.float32

    def chunks(t):
        return t.astype(f32).reshape(B, nc, HG_CHUNK, H, t.shape[-1]).transpose(1, 0, 3, 2, 4)

    xs = (chunks(q), chunks(k), chunks(v), chunks(g))
    incl = jnp.tril(jnp.ones((HG_CHUNK, HG_CHUNK), bool))[:, :, None]

    def step(state, inp):
        qt, kt, vt, gt = inp
        b = jnp.cumsum(gt, axis=2)
        diff = b[:, :, :, None, :] - b[:, :, None, :, :]
        decay = jnp.exp(jnp.where(incl, diff, -jnp.inf))
        a = jnp.einsum('bhtk,bhtsk,bhsk->bhts', qt, decay, kt)
        o = jnp.einsum('bhts,bhsv->bhtv', a, vt) + jnp.einsum('bhtk,bhkv->bhtv', qt * jnp.exp(b), state)
        b_last = b[:, :, -1:, :]
        state = jnp.exp(b_last[:, :, 0, :])[..., None] * state + jnp.einsum('bhsk,bhsv->bhkv', kt * jnp.exp(b_last - b), vt)
        return state, o

    s0 = jnp.zeros((B, H, K, V), f32)
    _, o = lax.scan(step, s0, xs)
    return o.transpose(1, 0, 3, 2, 4).reshape(B, S, H, V)


def hgrn2_bidirectional(hq, hi, hog, hf_fwd, hf_bwd, lb_fwd, lb_bwd, norm_g):
    B, S, _ = hq.shape
    f32 = jnp.float32
    q = jax.nn.silu(hq.astype(f32)).reshape(B, S, HG_HEADS, HG_KDIM)
    v = hi.reshape(B, S, HG_HEADS, HG_VDIM)

    def gates(fpre, lb):
        fpre = fpre.astype(f32)
        lb = lb.astype(f32)
        f = lb + (1.0 - lb) * jax.nn.sigmoid(fpre)
        k = (1.0 - lb) * jax.nn.sigmoid(-fpre)
        return k.reshape(B, S, HG_HEADS, HG_KDIM), jnp.log(f).reshape(B, S, HG_HEADS, HG_KDIM)

    k_f, g_f = gates(hf_fwd, lb_fwd)
    k_b, g_b = gates(hf_bwd, lb_bwd)
    flip = lambda t: jnp.flip(t, axis=1)
    o = hgrn2_scan(q, k_f, v, g_f) + flip(hgrn2_scan(flip(q), flip(k_b), flip(v), flip(g_b)))
    o = o * lax.rsqrt(jnp.mean(jnp.square(o), axis=-1, keepdims=True) + RMS_EPS)
    o = o.reshape(B, S, HG_V) * norm_g * jax.nn.silu(hog.astype(f32))
    return o.astype(hi.dtype)


def memory_attention(q, mem, w_mem_kv):
    B, S, _ = q.shape
    M = mem.shape[1]
    k, v = jnp.split(mem @ w_mem_kv, 2, axis=-1)
    q = q.reshape(B, S, MEM_HEADS, MEM_HEAD_DIM)
    k = k.reshape(B, M, MEM_HEADS, MEM_HEAD_DIM)
    v = v.reshape(B, M, MEM_HEADS, MEM_HEAD_DIM)
    s = jnp.einsum('bshd,bmhd->bhsm', q, k).astype(jnp.float32) * (MEM_HEAD_DIM ** -0.5)
    p = jax.nn.softmax(s, axis=-1).astype(v.dtype)
    return jnp.einsum('bhsm,bmhd->bshd', p, v).reshape(B, S, MEM_WIDTH)


def setup_inputs(seed: int = 0) -> dict:
    key = jax.random.key(seed)
    ks = jax.random.split(key, 20)
    f32 = jnp.float32
    nrm = lambda k, shape, s: jax.random.normal(k, shape, f32) * s
    col_scale = jnp.asarray(np.concatenate([np.full(n, s, np.float32) for n, s in zip(_IN_SIZES, _IN_COL_SCALE)]))
    mem_kv_scale = jnp.concatenate([jnp.ones((MEM_WIDTH,), f32), jnp.full((MEM_WIDTH,), BETA, f32)])
    return {
        "x": nrm(ks[0], (BATCH, SEQ, D_MODEL), 1.0),
        "mem": nrm(ks[1], (BATCH, MEM_LEN, D_MODEL), 1.0),
        "ln_emb_g": 1.0 + nrm(ks[2], (D_MODEL,), 0.02),
        "ln_emb_b": nrm(ks[3], (D_MODEL,), 0.02),
        "w_in": nrm(ks[4], (DEPTH, D_MODEL, IN_COLS), D_MODEL ** -0.5) * col_scale,
        "na_rpb": nrm(ks[5], (DEPTH, NA_HEADS, 2 * NA_KH_MAX - 1, 2 * NA_KW - 1), 0.02),
        "hg_lb_logits": nrm(ks[6], (2, DEPTH + 1, HG_QK), 0.5),
        "hg_norm_g": 1.0 + nrm(ks[7], (DEPTH, HG_V), 0.02),
        "w_mem_kv": nrm(ks[8], (DEPTH, D_MODEL, 2 * MEM_WIDTH), D_MODEL ** -0.5) * mem_kv_scale,
        "w_branch_na": nrm(ks[9], (DEPTH, NA_WIDTH, D_MODEL), BETA * NA_WIDTH ** -0.5),
        "w_branch_hg": nrm(ks[10], (DEPTH, HG_V, D_MODEL), BETA * HG_V ** -0.5),
        "w_branch_mem": nrm(ks[11], (DEPTH, MEM_WIDTH, D_MODEL), BETA * MEM_WIDTH ** -0.5),
        "w_out": nrm(ks[12], (DEPTH, D_MODEL, D_MODEL), BETA * D_MODEL ** -0.5),
        "ln1_g": 1.0 + nrm(ks[13], (DEPTH, D_MODEL), 0.02),
        "ln1_b": nrm(ks[14], (DEPTH, D_MODEL), 0.02),
        "w_ff1": nrm(ks[15], (DEPTH, D_MODEL, D_FF), BETA * D_MODEL ** -0.5),
        "w_ff2": nrm(ks[16], (DEPTH, D_FF, D_MODEL), BETA * D_FF ** -0.5),
        "ln2_g": 1.0 + nrm(ks[17], (DEPTH, D_MODEL), 0.02),
        "ln2_b": nrm(ks[18], (DEPTH, D_MODEL), 0.02),
    }


def reference(x, mem, ln_emb_g, ln_emb_b, w_in, na_rpb, hg_lb_logits, hg_norm_g, w_mem_kv,
              w_branch_na, w_branch_hg, w_branch_mem, w_out, ln1_g, ln1_b, w_ff1, w_ff2, ln2_g, ln2_b):
    B, S, _ = x.shape
    lb_all = jnp.cumsum(jax.nn.softmax(hg_lb_logits.astype(jnp.float32), axis=1), axis=1)
    x = layer_norm(x, ln_emb_g, ln_emb_b)
    for l in range(DEPTH):
        proj = x @ w_in[l]
        na_q, na_k, na_v, hg_q, hg_i, hg_og, hg_ff, hg_fb, mem_q, gates = jnp.split(proj, _split_points(), axis=-1)
        shp = (B, S, NA_HEADS, NA_HEAD_DIM)
        y_na = neighbourhood_attention(na_q.reshape(shp), na_k.reshape(shp), na_v.reshape(shp), na_rpb[l])
        y_hg = hgrn2_bidirectional(hg_q, hg_i, hg_og, hg_ff, hg_fb, lb_all[0, l], lb_all[1, l], hg_norm_g[l])
        y_mem = memory_attention(mem_q, mem, w_mem_kv[l])
        g_na, g_hg, g_mem = jnp.split(jax.nn.sigmoid(gates.astype(jnp.float32)).astype(x.dtype), N_BRANCH, axis=-1)
        merged = g_na * (y_na @ w_branch_na[l]) + g_hg * (y_hg @ w_branch_hg[l]) + g_mem * (y_mem @ w_branch_mem[l])
        x = layer_norm(ALPHA * x + merged @ w_out[l], ln1_g[l], ln1_b[l])
        h = jnp.square(jax.nn.relu(x @ w_ff1[l]))
        x = layer_norm(ALPHA * x + h @ w_ff2[l], ln2_g[l], ln2_b[l])
    return x
```

```python
import functools

import numpy as np
import jax
import jax.numpy as jnp
from jax import lax
from jax.experimental import pallas as pl
from jax.experimental.pallas import tpu as pltpu

F32 = jnp.float32
BF16 = jnp.bfloat16

D_MODEL = 1024
GRID_W = 64
NA_HEADS = 8
NA_HEAD_DIM = 64
NA_WIDTH = NA_HEADS * NA_HEAD_DIM
NA_KH = 8
NA_KW = 16
HG_HEADS = 4
HG_DIM = 128
HG_WIDTH = HG_HEADS * HG_DIM
MEM_HEADS = 4
MEM_HEAD_DIM = 128
MEM_WIDTH = MEM_HEADS * MEM_HEAD_DIM
D_FF = 4 * D_MODEL
DEPTH = 1
ALPHA = (2.0 * DEPTH) ** 0.25
LN_EPS = 1e-5
RMS_EPS = 1e-6
BRANCH_COLS = 9 * 512
MASK_NEG = -1e30

LANES = 128
SUBLANES = 8
HG_CHUNK = 256
VMEM_LIMIT = 56 * 1024 * 1024


def _layer_norm(x, g, b):
    mu = jnp.mean(x, axis=-1, keepdims=True)
    xc = x - mu
    var = jnp.mean(xc * xc, axis=-1, keepdims=True)
    return xc * lax.rsqrt(var + LN_EPS) * g + b


def _dot(a, b):
    return jnp.dot(a, b, preferred_element_type=F32)


def _dot_nt(a, b):
    return lax.dot_general(a, b, (((1,), (1,)), ((), ())), preferred_element_type=F32)


def _dot_tn(a, b):
    return lax.dot_general(a, b, (((0,), (0,)), ((), ())), preferred_element_type=F32)


def _ln_proj_kernel(x_ref, g_ref, b_ref, w_ref, qkv_ref, hq_ref, hi_ref, hog_ref, hff_ref, hfb_ref, mq_ref):
    xb = _layer_norm(x_ref[...], g_ref[...], b_ref[...]).astype(BF16)

    def cols(j):
        return _dot(xb, w_ref[:, j * 512:(j + 1) * 512])

    for j in range(3):
        qkv_ref[:, j * 512:(j + 1) * 512] = cols(j).astype(BF16)
    for j, ref in ((3, hq_ref), (4, hi_ref), (6, hff_ref), (7, hfb_ref)):
        r = cols(j)
        for h in range(HG_HEADS):
            ref[h] = r[:, h * HG_DIM:(h + 1) * HG_DIM].astype(ref.dtype)
    hog_ref[...] = cols(5)
    mq_ref[...] = cols(8).astype(BF16)


def _ln_proj(x2, g, b, w_br, tm):
    T = x2.shape[0]
    head_spec = pl.BlockSpec((HG_HEADS, tm, HG_DIM), lambda i: (0, i, 0))
    head_f32 = jax.ShapeDtypeStruct((HG_HEADS, T, HG_DIM), F32)
    return pl.pallas_call(
        _ln_proj_kernel,
        grid=(T // tm,),
        in_specs=[
            pl.BlockSpec((tm, D_MODEL), lambda i: (i, 0)),
            pl.BlockSpec((1, D_MODEL), lambda i: (0, 0)),
            pl.BlockSpec((1, D_MODEL), lambda i: (0, 0)),
            pl.BlockSpec((D_MODEL, BRANCH_COLS), lambda i: (0, 0)),
        ],
        out_specs=[
            pl.BlockSpec((tm, 3 * NA_WIDTH), lambda i: (i, 0)),
            head_spec, head_spec,
            pl.BlockSpec((tm, HG_WIDTH), lambda i: (i, 0)),
            head_spec, head_spec,
            pl.BlockSpec((tm, MEM_WIDTH), lambda i: (i, 0)),
        ],
        out_shape=[
            jax.ShapeDtypeStruct((T, 3 * NA_WIDTH), BF16),
            head_f32,
            jax.ShapeDtypeStruct((HG_HEADS, T, HG_DIM), BF16),
            jax.ShapeDtypeStruct((T, HG_WIDTH), F32),
            head_f32, head_f32,
            jax.ShapeDtypeStruct((T, MEM_WIDTH), BF16),
        ],
        compiler_params=pltpu.CompilerParams(
            dimension_semantics=("parallel",), vmem_limit_bytes=VMEM_LIMIT),
        name="ln_proj",
    )(x2, g, b, w_br)


def _na_bias_table(rpb):
    qc = np.arange(GRID_W)[:, None]
    kc = np.arange(GRID_W)[None, :]
    c0 = np.clip(qc - NA_KW // 2, 0, GRID_W - NA_KW)
    valid = (kc >= c0) & (kc < c0 + NA_KW)
    dc = np.clip(kc - qc + NA_KW - 1, 0, 2 * NA_KW - 2)
    off = np.arange(NA_KH)[:, None]
    a = np.arange(NA_KH)[None, :]
    dr = a - off + NA_KH - 1
    bias = rpb.astype(F32)[:, dr[:, :, None, None], dc[None, None, :, :]]
    bias = jnp.where(valid[None, None, None], bias, MASK_NEG)
    return bias.transpose(0, 1, 3, 2, 4).reshape(NA_HEADS, NA_KH, GRID_W, NA_KH * GRID_W)


def _na_kernel(q_ref, k_ref, v_ref, tab_ref, o_ref, *, rows):
    scale = NA_HEAD_DIM ** -0.5
    lane = lax.broadcasted_iota(jnp.int32, (GRID_W, LANES), 1)
    first = lane < NA_HEAD_DIM

    def row(r, carry):
        rs = jnp.clip(r - NA_KH // 2, 0, rows - NA_KH)
        off = r - rs
        q = q_ref[0, pl.ds(pl.multiple_of(r * GRID_W, GRID_W), GRID_W), :]
        kk = k_ref[0, pl.ds(pl.multiple_of(rs * GRID_W, GRID_W), NA_KH * GRID_W), :]
        vv = v_ref[0, pl.ds(pl.multiple_of(rs * GRID_W, GRID_W), NA_KH * GRID_W), :]
        outs = []
        for h in range(2):
            qh = jnp.where(first if h == 0 else jnp.logical_not(first), q, jnp.zeros_like(q))
            s = _dot_nt(qh, kk) * scale + tab_ref[h, off]
            m = jnp.max(s, axis=-1, keepdims=True)
            p = jnp.exp(s - m)
            l = jnp.sum(p, axis=-1, keepdims=True)
            outs.append(_dot(p.astype(BF16), vv) / l)
        o = jnp.where(first, outs[0], outs[1])
        o_ref[0, pl.ds(pl.multiple_of(r * GRID_W, GRID_W), GRID_W), :] = o.astype(o_ref.dtype)
        return carry

    lax.fori_loop(0, rows, row, 0)


def _na_attn(qkv3, tab):
    B, S, _ = qkv3.shape
    rows = S // GRID_W
    n_pairs = NA_HEADS // 2
    blk = lambda c0: pl.BlockSpec((1, S, LANES), lambda hp, b: (b, 0, c0 + hp))
    return pl.pallas_call(
        functools.partial(_na_kernel, rows=rows),
        grid=(n_pairs, B),
        in_specs=[
            blk(0), blk(n_pairs), blk(2 * n_pairs),
            pl.BlockSpec((2, NA_KH, GRID_W, NA_KH * GRID_W), lambda hp, b: (hp, 0, 0, 0)),
        ],
        out_specs=pl.BlockSpec((1, S, LANES), lambda hp, b: (b, 0, hp)),
        out_shape=jax.ShapeDtypeStruct((B, S, NA_WIDTH), BF16),
        compiler_params=pltpu.CompilerParams(
            dimension_semantics=("parallel", "parallel"), vmem_limit_bytes=VMEM_LIMIT),
        name="na_attn",
    )(qkv3, qkv3, qkv3, tab)


def _hgrn_scales_step(row, col, n, reverse):
    C = row.shape[0]
    nv = C // SUBLANES
    if n < SUBLANES:
        r3 = row.reshape(nv, SUBLANES, LANES)
        c3 = col.reshape(nv, SUBLANES, LANES)
        sub = lax.broadcasted_iota(jnp.int32, (nv, SUBLANES, LANES), 1)
        in_g = (sub & n) != 0
        bc = lambda i: jnp.broadcast_to(r3[:, i:i + 1, :], r3.shape)
        if n == 1:
            up = pltpu.roll(r3, 1, axis=1)
            dn = pltpu.roll(r3, SUBLANES - 1, axis=1)
            if not reverse:
                new_r = jnp.where(in_g, r3 * up, r3)
                new_c = jnp.where(in_g, c3, c3 * dn)
            else:
                new_r = jnp.where(in_g, r3, r3 * dn)
                new_c = jnp.where(in_g, c3 * up, c3)
        else:
            lo = sub < 4
            if not reverse:
                if n == 2:
                    tf = jnp.where(lo, bc(1), bc(5))
                    tg = jnp.where(lo, bc(3), bc(7))
                else:
                    tf, tg = bc(3), bc(7)
            else:
                if n == 2:
                    tf = jnp.where(lo, bc(0), bc(4))
                    tg = jnp.where(lo, bc(2), bc(6))
                else:
                    tf, tg = bc(0), bc(4)
            if not reverse:
                new_r = jnp.where(in_g, r3 * tf, r3)
                new_c = jnp.where(in_g, c3, c3 * tg)
            else:
                new_r = jnp.where(in_g, r3, r3 * tg)
                new_c = jnp.where(in_g, c3 * tf, c3)
        return new_r.reshape(C, LANES), new_c.reshape(C, LANES)

    m = n // SUBLANES
    nb = nv // (2 * m)
    r5 = row.reshape(nb, 2, m, SUBLANES, LANES)
    c5 = col.reshape(nb, 2, m, SUBLANES, LANES)
    rf, rg = r5[:, 0], r5[:, 1]
    cf, cg = c5[:, 0], c5[:, 1]
    if not reverse:
        tf = rf[:, m - 1:m, SUBLANES - 1:SUBLANES, :]
        tg = rg[:, m - 1:m, SUBLANES - 1:SUBLANES, :]
        new_r = jnp.stack([rf, rg * tf], axis=1)
        new_c = jnp.stack([cf * tg, cg], axis=1)
    else:
        tf = rf[:, 0:1, 0:1, :]
        tg = rg[:, 0:1, 0:1, :]
        new_r = jnp.stack([rf * tg, rg], axis=1)
        new_c = jnp.stack([cf, cg * tf], axis=1)
    return new_r.reshape(C, LANES), new_c.reshape(C, LANES)


def _hgrn_direction(q, fpre, lb, v_bf, state_ref, level, reverse):
    C = q.shape[0]
    sig = jax.nn.sigmoid(fpre)
    f = lb + (1.0 - lb) * sig
    k = (1.0 - lb) * jax.nn.sigmoid(-fpre)

    a = jnp.where(level == 8, _dot_nt(q.astype(BF16), k.astype(BF16)), 0.0)
    row, col = f, jnp.ones_like(f)
    n = 1
    for lv in range(8):
        p = _dot_nt((q * row).astype(BF16), (k * col).astype(BF16))
        a = jnp.where(level == lv, p, a)
        row, col = _hgrn_scales_step(row, col, n, reverse)
        n *= 2

    st = state_ref[...]
    o = _dot(a.astype(BF16), v_bf) + _dot_nt((q * row).astype(BF16), st.astype(BF16))
    total = row[C - 1:C, :] if not reverse else row[0:1, :]
    state_ref[...] = st * total + _dot_tn(v_bf, (k * col).astype(BF16))
    return o


def _hgrn_kernel(qf_ref, vf_ref, ff_ref, qb_ref, vb_ref, fb_ref, lbf_ref, lbb_ref,
                 of_ref, ob_ref, sf_ref, sb_ref):
    @pl.when(pl.program_id(2) == 0)
    def _():
        sf_ref[...] = jnp.zeros_like(sf_ref)
        sb_ref[...] = jnp.zeros_like(sb_ref)

    C = HG_CHUNK
    t = lax.broadcasted_iota(jnp.int32, (C, C), 0)
    s = lax.broadcasted_iota(jnp.int32, (C, C), 1)
    x = t ^ s
    hb = jnp.full((C, C), 8, jnp.int32)
    for lv in range(8):
        hb = jnp.where((x >> lv) == 1, lv, hb)
    level_f = jnp.where(t >= s, hb, -1)
    level_b = jnp.where(t <= s, hb, -1)

    qf = jax.nn.silu(qf_ref[0])
    of_ref[0] = _hgrn_direction(qf, ff_ref[0], lbf_ref[0], vf_ref[0], sf_ref, level_f, False)
    qb = jax.nn.silu(qb_ref[0])
    ob_ref[0] = _hgrn_direction(qb, fb_ref[0], lbb_ref[0], vb_ref[0], sb_ref, level_b, True)


def _hgrn(hq, hi, hff, hfb, lbf, lbb, batch):
    H, T, _ = hq.shape
    C = HG_CHUNK
    nc = T // batch // C
    fwd = pl.BlockSpec((1, C, HG_DIM), lambda b, h, c: (h, b * nc + c, 0))
    bwd = pl.BlockSpec((1, C, HG_DIM), lambda b, h, c: (h, b * nc + nc - 1 - c, 0))
    lb_spec = pl.BlockSpec((1, 1, HG_DIM), lambda b, h, c: (h, 0, 0))
    out = jax.ShapeDtypeStruct((H, T, HG_DIM), F32)
    return pl.pallas_call(
        _hgrn_kernel,
        grid=(batch, H, nc),
        in_specs=[fwd, fwd, fwd, bwd, bwd, bwd, lb_spec, lb_spec],
        out_specs=[fwd, bwd],
        out_shape=[out, out],
        scratch_shapes=[pltpu.VMEM((HG_DIM, HG_DIM), F32), pltpu.VMEM((HG_DIM, HG_DIM), F32)],
        compiler_params=pltpu.CompilerParams(
            dimension_semantics=("parallel", "parallel", "arbitrary"), vmem_limit_bytes=VMEM_LIMIT),
        name="hgrn",
    )(hq, hi, hff, hq, hi, hfb, lbf, lbb)


def _mem_attn_kernel(q_ref, mem_ref, wkv_ref, o_ref, k_scr, v_scr):
    @pl.when(pl.program_id(1) == 0)
    def _():
        mb = mem_ref[0].astype(BF16)
        k_scr[...] = _dot(mb, wkv_ref[:, :MEM_WIDTH]).astype(BF16)
        v_scr[...] = _dot(mb, wkv_ref[:, MEM_WIDTH:]).astype(BF16)

    scale = MEM_HEAD_DIM ** -0.5
    for h in range(MEM_HEADS):
        sl = slice(h * MEM_HEAD_DIM, (h + 1) * MEM_HEAD_DIM)
        s = _dot_nt(q_ref[:, sl], k_scr[:, sl]) * scale
        m = jnp.max(s, axis=-1, keepdims=True)
        p = jnp.exp(s - m)
        l = jnp.sum(p, axis=-1, keepdims=True)
        o_ref[:, sl] = (_dot(p.astype(BF16), v_scr[:, sl]) / l).astype(o_ref.dtype)


def _mem_attn(mq, mem, wkv, seq, tm):
    B, M, _ = mem.shape
    nt = seq // tm
    return pl.pallas_call(
        _mem_attn_kernel,
        grid=(B, nt),
        in_specs=[
            pl.BlockSpec((tm, MEM_WIDTH), lambda b, i: (b * nt + i, 0)),
            pl.BlockSpec((1, M, D_MODEL), lambda b, i: (b, 0, 0)),
            pl.BlockSpec((D_MODEL, 2 * MEM_WIDTH), lambda b, i: (0, 0)),
        ],
        out_specs=pl.BlockSpec((tm, MEM_WIDTH), lambda b, i: (b * nt + i, 0)),
        out_shape=jax.ShapeDtypeStruct((B * seq, MEM_WIDTH), BF16),
        scratch_shapes=[pltpu.VMEM((M, MEM_WIDTH), BF16), pltpu.VMEM((M, MEM_WIDTH), BF16)],
        compiler_params=pltpu.CompilerParams(
            dimension_semantics=("parallel", "arbitrary"), vmem_limit_bytes=VMEM_LIMIT),
        name="mem_attn",
    )(mq, mem, wkv)


def _merge_kernel(x_ref, ge_ref, be_ref, yna_ref, of_ref, ob_ref, hog_ref, ng_ref, ymem_ref,
                  wg_ref, wna_ref, whg_ref, wmem_ref, wout_ref, g1_ref, b1_ref, o_ref):
    xn = _layer_norm(x_ref[...], ge_ref[...], be_ref[...])
    xb = xn.astype(BF16)

    heads = []
    for h in range(HG_HEADS):
        o = of_ref[h] + ob_ref[h]
        heads.append(o * lax.rsqrt(jnp.mean(o * o, axis=-1, keepdims=True) + RMS_EPS))
    y_hg = jnp.concatenate(heads, axis=-1) * ng_ref[...] * jax.nn.silu(hog_ref[...])

    def gate(j):
        return jax.nn.sigmoid(_dot(xb, wg_ref[:, j * D_MODEL:(j + 1) * D_MODEL]))

    merged = gate(0) * _dot(yna_ref[...], wna_ref[...])
    merged = merged + gate(1) * _dot(y_hg.astype(BF16), whg_ref[...])
    merged = merged + gate(2) * _dot(ymem_ref[...], wmem_ref[...])
    y = ALPHA * xn + _dot(merged.astype(BF16), wout_ref[...])
    o_ref[...] = _layer_norm(y, g1_ref[...], b1_ref[...])


def _merge(x2, ge, be, y_na, o_f, o_b, hog, ng, y_mem, wg, wna, whg, wmem, wout, g1, b1, tm):
    T = x2.shape[0]
    row = lambda n: pl.BlockSpec((tm, n), lambda i: (i, 0))
    vec = lambda n: pl.BlockSpec((1, n), lambda i: (0, 0))
    full = lambda a, b: pl.BlockSpec((a, b), lambda i: (0, 0))
    head = pl.BlockSpec((HG_HEADS, tm, HG_DIM), lambda i: (0, i, 0))
    return pl.pallas_call(
        _merge_kernel,
        grid=(T // tm,),
        in_specs=[
            row(D_MODEL), vec(D_MODEL), vec(D_MODEL),
            row(NA_WIDTH), head, head, row(HG_WIDTH), vec(HG_WIDTH), row(MEM_WIDTH),
            full(D_MODEL, 3 * D_MODEL), full(NA_WIDTH, D_MODEL), full(HG_WIDTH, D_MODEL),
            full(MEM_WIDTH, D_MODEL), full(D_MODEL, D_MODEL), vec(D_MODEL), vec(D_MODEL),
        ],
        out_specs=row(D_MODEL),
        out_shape=jax.ShapeDtypeStruct((T, D_MODEL), F32),
        compiler_params=pltpu.CompilerParams(
            dimension_semantics=("parallel",), vmem_limit_bytes=VMEM_LIMIT),
        name="merge",
    )(x2, ge, be, y_na, o_f, o_b, hog, ng, y_mem, wg, wna, whg, wmem, wout, g1, b1)


def _ffn_kernel(x_ref, w1_ref, w2_ref, g_ref, b_ref, o_ref):
    x = x_ref[...]
    xb = x.astype(BF16)
    acc = ALPHA * x
    for j in range(D_FF // D_MODEL):
        sl = slice(j * D_MODEL, (j + 1) * D_MODEL)
        h = jnp.maximum(_dot(xb, w1_ref[:, sl]), 0.0)
        acc = acc + _dot((h * h).astype(BF16), w2_ref[sl, :])
    o_ref[...] = _layer_norm(acc, g_ref[...], b_ref[...])


def _ffn(x1, w1, w2, g, b, tm):
    T = x1.shape[0]
    return pl.pallas_call(
        _ffn_kernel,
        grid=(T // tm,),
        in_specs=[
            pl.BlockSpec((tm, D_MODEL), lambda i: (i, 0)),
            pl.BlockSpec((D_MODEL, D_FF), lambda i: (0, 0)),
            pl.BlockSpec((D_FF, D_MODEL), lambda i: (0, 0)),
            pl.BlockSpec((1, D_MODEL), lambda i: (0, 0)),
            pl.BlockSpec((1, D_MODEL), lambda i: (0, 0)),
        ],
        out_specs=pl.BlockSpec((tm, D_MODEL), lambda i: (i, 0)),
        out_shape=jax.ShapeDtypeStruct((T, D_MODEL), F32),
        compiler_params=pltpu.CompilerParams(
            dimension_semantics=("parallel",), vmem_limit_bytes=VMEM_LIMIT),
        name="ffn",
    )(x1, w1, w2, g, b)


def kernel(x, mem, ln_emb_g, ln_emb_b, w_in, na_rpb, hg_lb_logits, hg_norm_g, w_mem_kv, w_branch_na,
           w_branch_hg, w_branch_mem, w_out, ln1_g, ln1_b, w_ff1, w_ff2, ln2_g, ln2_b):
    B, S, D = x.shape
    assert D == D_MODEL and S % HG_CHUNK == 0 and S % GRID_W == 0 and S // GRID_W >= NA_KH
    assert w_in.shape[0] == DEPTH
    T = B * S
    l = 0
    vec = lambda a: a.reshape(1, -1).astype(F32)

    w_in_b = w_in[l].astype(BF16)
    w_br = w_in_b[:, :BRANCH_COLS]
    w_gates = w_in_b[:, BRANCH_COLS:]
    lb_all = jnp.cumsum(jax.nn.softmax(hg_lb_logits.astype(F32), axis=1), axis=1)
    lbf = lb_all[0, l].reshape(HG_HEADS, 1, HG_DIM)
    lbb = lb_all[1, l].reshape(HG_HEADS, 1, HG_DIM)
    tab = _na_bias_table(na_rpb[l])

    x2 = x.reshape(T, D)
    ge, be = vec(ln_emb_g), vec(ln_emb_b)
    qkv, hq, hi, hog, hff, hfb, mq = _ln_proj(x2, ge, be, w_br, tm=512)
    y_na = _na_attn(qkv.reshape(B, S, 3 * NA_WIDTH), tab).reshape(T, NA_WIDTH)
    o_f, o_b = _hgrn(hq, hi, hff, hfb, lbf, lbb, B)
    y_mem = _mem_attn(mq, mem, w_mem_kv[l].astype(BF16), S, tm=512)
    x1 = _merge(x2, ge, be, y_na, o_f, o_b, hog, vec(hg_norm_g[l]), y_mem, w_gates,
                w_branch_na[l].astype(BF16), w_branch_hg[l].astype(BF16), w_branch_mem[l].astype(BF16),
                w_out[l].astype(BF16), vec(ln1_g[l]), vec(ln1_b[l]), tm=512)
    out = _ffn(x1, w_ff1[l].astype(BF16), w_ff2[l].astype(BF16), vec(ln2_g[l]), vec(ln2_b[l]), tm=512)
    return out.reshape(B, S, D)
```

```python
import functools

import numpy as np
import jax
import jax.numpy as jnp
from jax import lax
from jax.experimental import pallas as pl
from jax.experimental.pallas import tpu as pltpu

F32 = jnp.float32
BF16 = jnp.bfloat16

D_MODEL = 1024
GRID_W = 64
NA_HEADS = 8
NA_HEAD_DIM = 64
NA_WIDTH = NA_HEADS * NA_HEAD_DIM
NA_KH = 8
NA_KW = 16
HG_HEADS = 4
HG_DIM = 128
HG_WIDTH = HG_HEADS * HG_DIM
MEM_HEADS = 4
MEM_HEAD_DIM = 128
MEM_WIDTH = MEM_HEADS * MEM_HEAD_DIM
D_FF = 4 * D_MODEL
DEPTH = 1
ALPHA = (2.0 * DEPTH) ** 0.25
LN_EPS = 1e-5
RMS_EPS = 1e-6
BRANCH_COLS = 9 * 512
MASK_NEG = -1e30

LANES = 128
SUBLANES = 8
HG_CHUNK = 256
HG_LEVELS = 7
VMEM_LIMIT = 56 * 1024 * 1024


def _layer_norm(x, g, b):
    mu = jnp.mean(x, axis=-1, keepdims=True)
    xc = x - mu
    var = jnp.mean(xc * xc, axis=-1, keepdims=True)
    return xc * lax.rsqrt(var + LN_EPS) * g + b


def _dot(a, b):
    return jnp.dot(a, b, preferred_element_type=F32)


def _dot_nt(a, b):
    return lax.dot_general(a, b, (((1,), (1,)), ((), ())), preferred_element_type=F32)


def _dot_tn(a, b):
    return lax.dot_general(a, b, (((0,), (0,)), ((), ())), preferred_element_type=F32)


def _ln_proj_kernel(x_ref, g_ref, b_ref, w_ref, qkv_ref, hq_ref, hi_ref, hog_ref, hff_ref, hfb_ref, mq_ref):
    xb = _layer_norm(x_ref[...], g_ref[...], b_ref[...]).astype(BF16)

    def cols(j):
        return _dot(xb, w_ref[:, j * 512:(j + 1) * 512])

    for j in range(3):
        qkv_ref[:, j * 512:(j + 1) * 512] = cols(j).astype(BF16)
    for j, ref in ((3, hq_ref), (4, hi_ref), (6, hff_ref), (7, hfb_ref)):
        r = cols(j)
        for h in range(HG_HEADS):
            ref[h] = r[:, h * HG_DIM:(h + 1) * HG_DIM].astype(ref.dtype)
    hog_ref[...] = cols(5)
    mq_ref[...] = cols(8).astype(BF16)


def _ln_proj(x2, g, b, w_br, tm):
    T = x2.shape[0]
    head_spec = pl.BlockSpec((HG_HEADS, tm, HG_DIM), lambda i: (0, i, 0))
    head_f32 = jax.ShapeDtypeStruct((HG_HEADS, T, HG_DIM), F32)
    return pl.pallas_call(
        _ln_proj_kernel,
        grid=(T // tm,),
        in_specs=[
            pl.BlockSpec((tm, D_MODEL), lambda i: (i, 0)),
            pl.BlockSpec((1, D_MODEL), lambda i: (0, 0)),
            pl.BlockSpec((1, D_MODEL), lambda i: (0, 0)),
            pl.BlockSpec((D_MODEL, BRANCH_COLS), lambda i: (0, 0)),
        ],
        out_specs=[
            pl.BlockSpec((tm, 3 * NA_WIDTH), lambda i: (i, 0)),
            head_spec, head_spec,
            pl.BlockSpec((tm, HG_WIDTH), lambda i: (i, 0)),
            head_spec, head_spec,
            pl.BlockSpec((tm, MEM_WIDTH), lambda i: (i, 0)),
        ],
        out_shape=[
            jax.ShapeDtypeStruct((T, 3 * NA_WIDTH), BF16),
            head_f32,
            jax.ShapeDtypeStruct((HG_HEADS, T, HG_DIM), BF16),
            jax.ShapeDtypeStruct((T, HG_WIDTH), F32),
            head_f32, head_f32,
            jax.ShapeDtypeStruct((T, MEM_WIDTH), BF16),
        ],
        compiler_params=pltpu.CompilerParams(
            dimension_semantics=("parallel",), vmem_limit_bytes=VMEM_LIMIT),
        name="ln_proj",
    )(x2, g, b, w_br)


def _na_bias_table(rpb):
    qc = np.arange(GRID_W)[:, None]
    kc = np.arange(GRID_W)[None, :]
    c0 = np.clip(qc - NA_KW // 2, 0, GRID_W - NA_KW)
    valid = (kc >= c0) & (kc < c0 + NA_KW)
    dc = kc - qc + NA_KW - 1
    onehot = ((dc[None] == np.arange(2 * NA_KW - 1)[:, None, None]) & valid[None]).astype(np.float32)
    bnd = jnp.einsum("hdj,jqk->hdqk", rpb.astype(F32), jnp.asarray(onehot), precision=lax.Precision.HIGHEST)
    bnd = bnd + jnp.asarray(np.where(valid, 0.0, MASK_NEG).astype(np.float32))
    pad = jnp.full((NA_HEADS, 4, GRID_W, GRID_W), MASK_NEG, F32)
    ext = jnp.concatenate([pad, bnd, pad], axis=1)
    left, right = ext[:, :-1], ext[:, 1:]
    m = jnp.full_like(left, MASK_NEG)
    cat = lambda a, b: jnp.concatenate([a, b], axis=-1)
    return jnp.concatenate([cat(left, right), cat(left, m), cat(m, right), cat(m, m)[:, :1]], axis=1)


NA_GROUP = 4
NA_WIN = 12
NA_PAIRS = 22


def _na_kernel(q_ref, k_ref, v_ref, tab_ref, o_ref, *, rows):
    scale = NA_HEAD_DIM ** -0.5
    gq = NA_GROUP * GRID_W
    first = lax.broadcasted_iota(jnp.int32, (gq, LANES), 1) < NA_HEAD_DIM
    first_row = lax.broadcasted_iota(jnp.int32, (GRID_W, LANES), 1) < NA_HEAD_DIM

    def group(g, carry):
        r0 = g * NA_GROUP
        ws = jnp.clip(r0 - NA_KH // 2, 0, rows - NA_WIN)
        q = q_ref[0, pl.ds(pl.multiple_of(r0 * GRID_W, gq), gq), :]
        kw = k_ref[0, pl.ds(pl.multiple_of(ws * GRID_W, GRID_W), NA_WIN * GRID_W), :]
        vw = v_ref[0, pl.ds(pl.multiple_of(ws * GRID_W, GRID_W), NA_WIN * GRID_W), :]
        zero = jnp.zeros_like(q)
        qs = jnp.concatenate([jnp.where(first, q, zero), jnp.where(first, zero, q)], axis=0)
        s_all = _dot_nt(qs, kw) * scale

        idx = []
        for j in range(NA_GROUP):
            r = r0 + j
            rs = jnp.clip(r - NA_KH // 2, 0, rows - NA_KH)
            row_idx = []
            for a2 in range(NA_WIN // 2):
                kr = ws + 2 * a2
                lv = jnp.logical_and(kr >= rs, kr < rs + NA_KH)
                rv = jnp.logical_and(kr + 1 >= rs, kr + 1 < rs + NA_KH)
                i = kr - r + (NA_KH - 1) + 4
                row_idx.append(jnp.where(jnp.logical_and(lv, rv), i,
                                         jnp.where(lv, NA_PAIRS + i,
                                                   jnp.where(rv, 2 * NA_PAIRS + i, 3 * NA_PAIRS))))
            idx.append(row_idx)

        ps, ls = [], []
        for h in range(2):
            for j in range(NA_GROUP):
                bias = jnp.concatenate([tab_ref[h, i] for i in idx[j]], axis=1)
                lo = (h * NA_GROUP + j) * GRID_W
                s = s_all[lo:lo + GRID_W] + bias
                m = jnp.max(s, axis=-1, keepdims=True)
                p = jnp.exp(s - m)
                ls.append(jnp.sum(p, axis=-1, keepdims=True))
                ps.append(p.astype(BF16))
        o_all = _dot(jnp.concatenate(ps, axis=0), vw)
        for j in range(NA_GROUP):
            o0 = o_all[j * GRID_W:(j + 1) * GRID_W] / ls[j]
            o1 = o_all[gq + j * GRID_W:gq + (j + 1) * GRID_W] / ls[NA_GROUP + j]
            o = jnp.where(first_row, o0, o1)
            o_ref[0, pl.ds(pl.multiple_of((r0 + j) * GRID_W, GRID_W), GRID_W), :] = o.astype(o_ref.dtype)
        return carry

    lax.fori_loop(0, rows // NA_GROUP, group, 0)


def _na_attn(qkv3, tab):
    B, S, _ = qkv3.shape
    rows = S // GRID_W
    assert rows % NA_GROUP == 0 and rows >= NA_WIN
    n_pairs = NA_HEADS // 2
    blk = lambda c0: pl.BlockSpec((1, S, LANES), lambda hp, b: (b, 0, c0 + hp))
    return pl.pallas_call(
        functools.partial(_na_kernel, rows=rows),
        grid=(n_pairs, B),
        in_specs=[
            blk(0), blk(n_pairs), blk(2 * n_pairs),
            pl.BlockSpec((2, 3 * NA_PAIRS + 1, GRID_W, LANES), lambda hp, b: (hp, 0, 0, 0)),
        ],
        out_specs=pl.BlockSpec((1, S, LANES), lambda hp, b: (b, 0, hp)),
        out_shape=jax.ShapeDtypeStruct((B, S, NA_WIDTH), BF16),
        compiler_params=pltpu.CompilerParams(
            dimension_semantics=("parallel", "parallel"), vmem_limit_bytes=VMEM_LIMIT),
        name="na_attn",
    )(qkv3, qkv3, qkv3, tab)


def _hgrn_scales_step(row, col, n, reverse):
    C = row.shape[0]
    nv = C // SUBLANES
    if n < SUBLANES:
        r3 = row.reshape(nv, SUBLANES, LANES)
        c3 = col.reshape(nv, SUBLANES, LANES)
        sub = lax.broadcasted_iota(jnp.int32, (nv, SUBLANES, LANES), 1)
        in_g = (sub & n) != 0
        bc = lambda i: jnp.broadcast_to(r3[:, i:i + 1, :], r3.shape)
        if n == 1:
            up = pltpu.roll(r3, 1, axis=1)
            dn = pltpu.roll(r3, SUBLANES - 1, axis=1)
            if not reverse:
                new_r = jnp.where(in_g, r3 * up, r3)
                new_c = jnp.where(in_g, c3, c3 * dn)
            else:
                new_r = jnp.where(in_g, r3, r3 * dn)
                new_c = jnp.where(in_g, c3 * up, c3)
        else:
            lo = sub < 4
            if not reverse:
                if n == 2:
                    tf = jnp.where(lo, bc(1), bc(5))
                    tg = jnp.where(lo, bc(3), bc(7))
                else:
                    tf, tg = bc(3), bc(7)
            else:
                if n == 2:
                    tf = jnp.where(lo, bc(0), bc(4))
                    tg = jnp.where(lo, bc(2), bc(6))
                else:
                    tf, tg = bc(0), bc(4)
            if not reverse:
                new_r = jnp.where(in_g, r3 * tf, r3)
                new_c = jnp.where(in_g, c3, c3 * tg)
            else:
                new_r = jnp.where(in_g, r3, r3 * tg)
                new_c = jnp.where(in_g, c3 * tf, c3)
        return new_r.reshape(C, LANES), new_c.reshape(C, LANES)

    m = n // SUBLANES
    nb = nv // (2 * m)
    r5 = row.reshape(nb, 2, m, SUBLANES, LANES)
    c5 = col.reshape(nb, 2, m, SUBLANES, LANES)
    rf, rg = r5[:, 0], r5[:, 1]
    cf, cg = c5[:, 0], c5[:, 1]
    if not reverse:
        tf = rf[:, m - 1:m, SUBLANES - 1:SUBLANES, :]
        tg = rg[:, m - 1:m, SUBLANES - 1:SUBLANES, :]
        new_r = jnp.stack([rf, rg * tf], axis=1)
        new_c = jnp.stack([cf * tg, cg], axis=1)
    else:
        tf = rf[:, 0:1, 0:1, :]
        tg = rg[:, 0:1, 0:1, :]
        new_r = jnp.stack([rf * tg, rg], axis=1)
        new_c = jnp.stack([cf, cg * tf], axis=1)
    return new_r.reshape(C, LANES), new_c.reshape(C, LANES)


def _hgrn_direction(q, fpre, lb, v_bf, state_ref, level, reverse):
    C = q.shape[0]
    hc = C // 2
    halves = (slice(0, hc), slice(hc, C))
    sig = jax.nn.sigmoid(fpre)
    f = lb + (1.0 - lb) * sig
    k = (1.0 - lb) * jax.nn.sigmoid(-fpre)

    qb, kb = q.astype(BF16), k.astype(BF16)
    a_diag = [jnp.where(level == HG_LEVELS, _dot_nt(qb[h], kb[h]), 0.0) for h in halves]
    row, col = f, jnp.ones_like(f)
    n = 1
    for lv in range(HG_LEVELS):
        qs, ks = (q * row).astype(BF16), (k * col).astype(BF16)
        a_diag = [jnp.where(level == lv, _dot_nt(qs[h], ks[h]), a) for h, a in zip(halves, a_diag)]
        row, col = _hgrn_scales_step(row, col, n, reverse)
        n *= 2

    qs, ks = (q * row).astype(BF16), (k * col).astype(BF16)
    lo, hi = halves
    if not reverse:
        a_x = _dot_nt(qs[hi], ks[lo])
        o_lo = _dot(a_diag[0].astype(BF16), v_bf[lo])
        o_hi = _dot(jnp.concatenate([a_x, a_diag[1]], axis=1).astype(BF16), v_bf)
    else:
        a_x = _dot_nt(qs[lo], ks[hi])
        o_lo = _dot(jnp.concatenate([a_diag[0], a_x], axis=1).astype(BF16), v_bf)
        o_hi = _dot(a_diag[1].astype(BF16), v_bf[hi])
    row, col = _hgrn_scales_step(row, col, n, reverse)

    st = state_ref[...]
    o = jnp.concatenate([o_lo, o_hi], axis=0) + _dot_nt((q * row).astype(BF16), st.astype(BF16))
    total = row[C - 1:C, :] if not reverse else row[0:1, :]
    state_ref[...] = st * total + _dot_tn(v_bf, (k * col).astype(BF16))
    return o


def _hgrn_kernel(qf_ref, vf_ref, ff_ref, qb_ref, vb_ref, fb_ref, lbf_ref, lbb_ref, lvf_ref, lvb_ref,
                 of_ref, ob_ref, sf_ref, sb_ref):
    @pl.when(pl.program_id(2) == 0)
    def _():
        sf_ref[...] = jnp.zeros_like(sf_ref)
        sb_ref[...] = jnp.zeros_like(sb_ref)

    qf = jax.nn.silu(qf_ref[0])
    of_ref[0] = _hgrn_direction(qf, ff_ref[0], lbf_ref[0], vf_ref[0], sf_ref, lvf_ref[...], False)
    qb = jax.nn.silu(qb_ref[0])
    ob_ref[0] = _hgrn_direction(qb, fb_ref[0], lbb_ref[0], vb_ref[0], sb_ref, lvb_ref[...], True)


def _hgrn_level_tables():
    hc = HG_CHUNK // 2
    t = np.arange(hc)[:, None]
    s = np.arange(hc)[None, :]
    x = t ^ s
    hb = np.where(x > 0, np.floor(np.log2(np.maximum(x, 1))), HG_LEVELS).astype(np.int32)
    return jnp.asarray(np.where(t >= s, hb, -1)), jnp.asarray(np.where(t <= s, hb, -1))


def _hgrn(hq, hi, hff, hfb, lbf, lbb, batch):
    H, T, _ = hq.shape
    C = HG_CHUNK
    nc = T // batch // C
    fwd = pl.BlockSpec((1, C, HG_DIM), lambda b, h, c: (h, b * nc + c, 0))
    bwd = pl.BlockSpec((1, C, HG_DIM), lambda b, h, c: (h, b * nc + nc - 1 - c, 0))
    lb_spec = pl.BlockSpec((1, 1, HG_DIM), lambda b, h, c: (h, 0, 0))
    lv_spec = pl.BlockSpec((C // 2, C // 2), lambda b, h, c: (0, 0))
    lvf, lvb = _hgrn_level_tables()
    out = jax.ShapeDtypeStruct((H, T, HG_DIM), F32)
    return pl.pallas_call(
        _hgrn_kernel,
        grid=(batch, H, nc),
        in_specs=[fwd, fwd, fwd, bwd, bwd, bwd, lb_spec, lb_spec, lv_spec, lv_spec],
        out_specs=[fwd, bwd],
        out_shape=[out, out],
        scratch_shapes=[pltpu.VMEM((HG_DIM, HG_DIM), F32), pltpu.VMEM((HG_DIM, HG_DIM), F32)],
        compiler_params=pltpu.CompilerParams(
            dimension_semantics=("parallel", "parallel", "arbitrary"), vmem_limit_bytes=VMEM_LIMIT),
        name="hgrn",
    )(hq, hi, hff, hq, hi, hfb, lbf, lbb, lvf, lvb)


def _mem_attn_kernel(q_ref, mem_ref, wkv_ref, o_ref, k_scr, v_scr):
    @pl.when(pl.program_id(1) == 0)
    def _():
        mb = mem_ref[0].astype(BF16)
        k_scr[...] = _dot(mb, wkv_ref[:, :MEM_WIDTH]).astype(BF16)
        v_scr[...] = _dot(mb, wkv_ref[:, MEM_WIDTH:]).astype(BF16)

    scale = MEM_HEAD_DIM ** -0.5
    for h in range(MEM_HEADS):
        sl = slice(h * MEM_HEAD_DIM, (h + 1) * MEM_HEAD_DIM)
        s = _dot_nt(q_ref[:, sl], k_scr[:, sl]) * scale
        m = jnp.max(s, axis=-1, keepdims=True)
        p = jnp.exp(s - m)
        l = jnp.sum(p, axis=-1, keepdims=True)
        o_ref[:, sl] = (_dot(p.astype(BF16), v_scr[:, sl]) / l).astype(o_ref.dtype)


def _mem_attn(mq, mem, wkv, seq, tm):
    B, M, _ = mem.shape
    nt = seq // tm
    return pl.pallas_call(
        _mem_attn_kernel,
        grid=(B, nt),
        in_specs=[
            pl.BlockSpec((tm, MEM_WIDTH), lambda b, i: (b * nt + i, 0)),
            pl.BlockSpec((1, M, D_MODEL), lambda b, i: (b, 0, 0)),
            pl.BlockSpec((D_MODEL, 2 * MEM_WIDTH), lambda b, i: (0, 0)),
        ],
        out_specs=pl.BlockSpec((tm, MEM_WIDTH), lambda b, i: (b * nt + i, 0)),
        out_shape=jax.ShapeDtypeStruct((B * seq, MEM_WIDTH), BF16),
        scratch_shapes=[pltpu.VMEM((M, MEM_WIDTH), BF16), pltpu.VMEM((M, MEM_WIDTH), BF16)],
        compiler_params=pltpu.CompilerParams(
            dimension_semantics=("parallel", "arbitrary"), vmem_limit_bytes=VMEM_LIMIT),
        name="mem_attn",
    )(mq, mem, wkv)


def _merge_kernel(x_ref, ge_ref, be_ref, yna_ref, of_ref, ob_ref, hog_ref, ng_ref, ymem_ref,
                  wg_ref, wna_ref, whg_ref, wmem_ref, wout_ref, g1_ref, b1_ref, o_ref):
    xn = _layer_norm(x_ref[...], ge_ref[...], be_ref[...])
    xb = xn.astype(BF16)

    heads = []
    for h in range(HG_HEADS):
        o = of_ref[h] + ob_ref[h]
        heads.append(o * lax.rsqrt(jnp.mean(o * o, axis=-1, keepdims=True) + RMS_EPS))
    y_hg = jnp.concatenate(heads, axis=-1) * ng_ref[...] * jax.nn.silu(hog_ref[...])

    def gate(j):
        return jax.nn.sigmoid(_dot(xb, wg_ref[:, j * D_MODEL:(j + 1) * D_MODEL]))

    merged = gate(0) * _dot(yna_ref[...], wna_ref[...])
    merged = merged + gate(1) * _dot(y_hg.astype(BF16), whg_ref[...])
    merged = merged + gate(2) * _dot(ymem_ref[...], wmem_ref[...])
    y = ALPHA * xn + _dot(merged.astype(BF16), wout_ref[...])
    o_ref[...] = _layer_norm(y, g1_ref[...], b1_ref[...])


def _merge(x2, ge, be, y_na, o_f, o_b, hog, ng, y_mem, wg, wna, whg, wmem, wout, g1, b1, tm):
    T = x2.shape[0]
    row = lambda n: pl.BlockSpec((tm, n), lambda i: (i, 0))
    vec = lambda n: pl.BlockSpec((1, n), lambda i: (0, 0))
    full = lambda a, b: pl.BlockSpec((a, b), lambda i: (0, 0))
    head = pl.BlockSpec((HG_HEADS, tm, HG_DIM), lambda i: (0, i, 0))
    return pl.pallas_call(
        _merge_kernel,
        grid=(T // tm,),
        in_specs=[
            row(D_MODEL), vec(D_MODEL), vec(D_MODEL),
            row(NA_WIDTH), head, head, row(HG_WIDTH), vec(HG_WIDTH), row(MEM_WIDTH),
            full(D_MODEL, 3 * D_MODEL), full(NA_WIDTH, D_MODEL), full(HG_WIDTH, D_MODEL),
            full(MEM_WIDTH, D_MODEL), full(D_MODEL, D_MODEL), vec(D_MODEL), vec(D_MODEL),
        ],
        out_specs=row(D_MODEL),
        out_shape=jax.ShapeDtypeStruct((T, D_MODEL), F32),
        compiler_params=pltpu.CompilerParams(
            dimension_semantics=("parallel",), vmem_limit_bytes=VMEM_LIMIT),
        name="merge",
    )(x2, ge, be, y_na, o_f, o_b, hog, ng, y_mem, wg, wna, whg, wmem, wout, g1, b1)


def _ffn_kernel(x_ref, w1_ref, w2_ref, g_ref, b_ref, o_ref):
    x = x_ref[...]
    xb = x.astype(BF16)
    acc = ALPHA * x
    for j in range(D_FF // D_MODEL):
        sl = slice(j * D_MODEL, (j + 1) * D_MODEL)
        h = jnp.maximum(_dot(xb, w1_ref[:, sl]), 0.0)
        acc = acc + _dot((h * h).astype(BF16), w2_ref[sl, :])
    o_ref[...] = _layer_norm(acc, g_ref[...], b_ref[...])


def _ffn(x1, w1, w2, g, b, tm):
    T = x1.shape[0]
    return pl.pallas_call(
        _ffn_kernel,
        grid=(T // tm,),
        in_specs=[
            pl.BlockSpec((tm, D_MODEL), lambda i: (i, 0)),
            pl.BlockSpec((D_MODEL, D_FF), lambda i: (0, 0)),
            pl.BlockSpec((D_FF, D_MODEL), lambda i: (0, 0)),
            pl.BlockSpec((1, D_MODEL), lambda i: (0, 0)),
            pl.BlockSpec((1, D_MODEL), lambda i: (0, 0)),
        ],
        out_specs=pl.BlockSpec((tm, D_MODEL), lambda i: (i, 0)),
        out_shape=jax.ShapeDtypeStruct((T, D_MODEL), F32),
        compiler_params=pltpu.CompilerParams(
            dimension_semantics=("parallel",), vmem_limit_bytes=VMEM_LIMIT),
        name="ffn",
    )(x1, w1, w2, g, b)


def kernel(x, mem, ln_emb_g, ln_emb_b, w_in, na_rpb, hg_lb_logits, hg_norm_g, w_mem_kv, w_branch_na,
           w_branch_hg, w_branch_mem, w_out, ln1_g, ln1_b, w_ff1, w_ff2, ln2_g, ln2_b):
    B, S, D = x.shape
    assert D == D_MODEL and S % HG_CHUNK == 0 and S % GRID_W == 0 and S // GRID_W >= NA_KH
    assert w_in.shape[0] == DEPTH
    T = B * S
    l = 0
    vec = lambda a: a.reshape(1, -1).astype(F32)

    w_in_b = w_in[l].astype(BF16)
    w_br = w_in_b[:, :BRANCH_COLS]
    w_gates = w_in_b[:, BRANCH_COLS:]
    lb_all = jnp.cumsum(jax.nn.softmax(hg_lb_logits.astype(F32), axis=1), axis=1)
    lbf = lb_all[0, l].reshape(HG_HEADS, 1, HG_DIM)
    lbb = lb_all[1, l].reshape(HG_HEADS, 1, HG_DIM)
    tab = _na_bias_table(na_rpb[l])

    x2 = x.reshape(T, D)
    ge, be = vec(ln_emb_g), vec(ln_emb_b)
    qkv, hq, hi, hog, hff, hfb, mq = _ln_proj(x2, ge, be, w_br, tm=512)
    y_na = _na_attn(qkv.reshape(B, S, 3 * NA_WIDTH), tab).reshape(T, NA_WIDTH)
    o_f, o_b = _hgrn(hq, hi, hff, hfb, lbf, lbb, B)
    y_mem = _mem_attn(mq, mem, w_mem_kv[l].astype(BF16), S, tm=512)
    x1 = _merge(x2, ge, be, y_na, o_f, o_b, hog, vec(hg_norm_g[l]), y_mem, w_gates,
                w_branch_na[l].astype(BF16), w_branch_hg[l].astype(BF16), w_branch_mem[l].astype(BF16),
                w_out[l].astype(BF16), vec(ln1_g[l]), vec(ln1_b[l]), tm=512)
    out = _ffn(x1, w_ff1[l].astype(BF16), w_ff2[l].astype(BF16), vec(ln2_g[l]), vec(ln2_b[l]), tm=512)
    return out.reshape(B, S, D)
```

```python
import functools

import numpy as np
import jax
import jax.numpy as jnp
from jax import lax
from jax.experimental import pallas as pl
from jax.experimental.pallas import tpu as pltpu

F32 = jnp.float32
BF16 = jnp.bfloat16

D_MODEL = 1024
GRID_W = 64
NA_HEADS = 8
NA_HEAD_DIM = 64
NA_WIDTH = NA_HEADS * NA_HEAD_DIM
NA_KH = 8
NA_KW = 16
HG_HEADS = 4
HG_DIM = 128
HG_WIDTH = HG_HEADS * HG_DIM
MEM_HEADS = 4
MEM_HEAD_DIM = 128
MEM_WIDTH = MEM_HEADS * MEM_HEAD_DIM
D_FF = 4 * D_MODEL
DEPTH = 1
ALPHA = (2.0 * DEPTH) ** 0.25
LN_EPS = 1e-5
RMS_EPS = 1e-6
BRANCH_COLS = 9 * 512
MASK_NEG = -1e30

LANES = 128
SUBLANES = 8
HG_CHUNK = 256
HG_HPS = 2
HG_LEVELS = 7
VMEM_LIMIT = 56 * 1024 * 1024


def _layer_norm(x, g, b):
    mu = jnp.mean(x, axis=-1, keepdims=True)
    xc = x - mu
    var = jnp.mean(xc * xc, axis=-1, keepdims=True)
    return xc * lax.rsqrt(var + LN_EPS) * g + b


def _dot(a, b):
    return jnp.dot(a, b, preferred_element_type=F32)


def _dot_nt(a, b):
    return lax.dot_general(a, b, (((1,), (1,)), ((), ())), preferred_element_type=F32)


def _dot_tn(a, b):
    return lax.dot_general(a, b, (((0,), (0,)), ((), ())), preferred_element_type=F32)


def _ln_proj_kernel(x_ref, g_ref, b_ref, w_ref, qkv_ref, hq_ref, hi_ref, hog_ref, hff_ref, hfb_ref, mq_ref):
    xb = _layer_norm(x_ref[...], g_ref[...], b_ref[...]).astype(BF16)

    def cols(j):
        return _dot(xb, w_ref[:, j * 512:(j + 1) * 512])

    for j in range(3):
        qkv_ref[:, j * 512:(j + 1) * 512] = cols(j).astype(BF16)
    for j, ref in ((3, hq_ref), (4, hi_ref), (6, hff_ref), (7, hfb_ref)):
        r = cols(j)
        for h in range(HG_HEADS):
            ref[h] = r[:, h * HG_DIM:(h + 1) * HG_DIM].astype(ref.dtype)
    hog_ref[...] = cols(5)
    mq_ref[...] = cols(8).astype(BF16)


def _ln_proj(x2, g, b, w_br, tm):
    T = x2.shape[0]
    head_spec = pl.BlockSpec((HG_HEADS, tm, HG_DIM), lambda i: (0, i, 0))
    head_f32 = jax.ShapeDtypeStruct((HG_HEADS, T, HG_DIM), F32)
    return pl.pallas_call(
        _ln_proj_kernel,
        grid=(T // tm,),
        in_specs=[
            pl.BlockSpec((tm, D_MODEL), lambda i: (i, 0)),
            pl.BlockSpec((1, D_MODEL), lambda i: (0, 0)),
            pl.BlockSpec((1, D_MODEL), lambda i: (0, 0)),
            pl.BlockSpec((D_MODEL, BRANCH_COLS), lambda i: (0, 0)),
        ],
        out_specs=[
            pl.BlockSpec((tm, 3 * NA_WIDTH), lambda i: (i, 0)),
            head_spec, head_spec,
            pl.BlockSpec((tm, HG_WIDTH), lambda i: (i, 0)),
            head_spec, head_spec,
            pl.BlockSpec((tm, MEM_WIDTH), lambda i: (i, 0)),
        ],
        out_shape=[
            jax.ShapeDtypeStruct((T, 3 * NA_WIDTH), BF16),
            head_f32,
            jax.ShapeDtypeStruct((HG_HEADS, T, HG_DIM), BF16),
            jax.ShapeDtypeStruct((T, HG_WIDTH), F32),
            head_f32, head_f32,
            jax.ShapeDtypeStruct((T, MEM_WIDTH), BF16),
        ],
        compiler_params=pltpu.CompilerParams(
            dimension_semantics=("parallel",), vmem_limit_bytes=VMEM_LIMIT),
        name="ln_proj",
    )(x2, g, b, w_br)


def _na_bias_table(rpb):
    qc = np.arange(GRID_W)[:, None]
    kc = np.arange(GRID_W)[None, :]
    c0 = np.clip(qc - NA_KW // 2, 0, GRID_W - NA_KW)
    valid = (kc >= c0) & (kc < c0 + NA_KW)
    dc = kc - qc + NA_KW - 1
    onehot = ((dc[None] == np.arange(2 * NA_KW - 1)[:, None, None]) & valid[None]).astype(np.float32)
    bnd = jnp.einsum("hdj,jqk->hdqk", rpb.astype(F32), jnp.asarray(onehot), precision=lax.Precision.HIGHEST)
    bnd = bnd + jnp.asarray(np.where(valid, 0.0, MASK_NEG).astype(np.float32))
    pad = jnp.full((NA_HEADS, 4, GRID_W, GRID_W), MASK_NEG, F32)
    ext = jnp.concatenate([pad, bnd, pad], axis=1)
    left, right = ext[:, :-1], ext[:, 1:]
    m = jnp.full_like(left, MASK_NEG)
    cat = lambda a, b: jnp.concatenate([a, b], axis=-1)
    return jnp.concatenate([cat(left, right), cat(left, m), cat(m, right), cat(m, m)[:, :1]], axis=1)


NA_GROUP = 4
NA_WIN = 12
NA_PAIRS = 22


def _na_kernel(q_ref, k_ref, v_ref, tab_ref, o_ref, *, rows):
    scale = NA_HEAD_DIM ** -0.5
    assert np.log2(scale) == np.round(np.log2(scale))
    gq = NA_GROUP * GRID_W
    first = lax.broadcasted_iota(jnp.int32, (gq, LANES), 1) < NA_HEAD_DIM
    first_row = lax.broadcasted_iota(jnp.int32, (GRID_W, LANES), 1) < NA_HEAD_DIM

    def group(g, carry):
        r0 = g * NA_GROUP
        ws = jnp.clip(r0 - NA_KH // 2, 0, rows - NA_WIN)
        q = q_ref[0, pl.ds(pl.multiple_of(r0 * GRID_W, gq), gq), :] * jnp.asarray(scale, BF16)
        kw =k_ref[0, pl.ds(pl.multiple_of(ws * GRID_W, GRID_W), NA_WIN * GRID_W), :]
        vw = v_ref[0, pl.ds(pl.multiple_of(ws * GRID_W, GRID_W), NA_WIN * GRID_W), :]
        zero = jnp.zeros_like(q)
        qs = jnp.concatenate([jnp.where(first, q, zero), jnp.where(first, zero, q)], axis=0)
        s_all = _dot_nt(qs, kw)

        idx = []
        for j in range(NA_GROUP):
            r = r0 + j
            rs = jnp.clip(r - NA_KH // 2, 0, rows - NA_KH)
            row_idx = []
            for a2 in range(NA_WIN // 2):
                kr = ws + 2 * a2
                lv = jnp.logical_and(kr >= rs, kr < rs + NA_KH)
                rv = jnp.logical_and(kr + 1 >= rs, kr + 1 < rs + NA_KH)
                i = kr - r + (NA_KH - 1) + 4
                row_idx.append(jnp.where(jnp.logical_and(lv, rv), i,
                                         jnp.where(lv, NA_PAIRS + i,
                                                   jnp.where(rv, 2 * NA_PAIRS + i, 3 * NA_PAIRS))))
            idx.append(row_idx)

        ps, ls = [], []
        for h in range(2):
            for j in range(NA_GROUP):
                bias = jnp.concatenate([tab_ref[h, i] for i in idx[j]], axis=1)
                lo = (h * NA_GROUP + j) * GRID_W
                s = s_all[lo:lo + GRID_W] + bias
                m = jnp.max(s, axis=-1, keepdims=True)
                p = jnp.exp(s - m)
                ls.append(jnp.sum(p, axis=-1, keepdims=True))
                ps.append(p.astype(BF16))
        o_all = _dot(jnp.concatenate(ps, axis=0), vw)
        for j in range(NA_GROUP):
            o0 = o_all[j * GRID_W:(j + 1) * GRID_W] / ls[j]
            o1 = o_all[gq + j * GRID_W:gq + (j + 1) * GRID_W] / ls[NA_GROUP + j]
            o = jnp.where(first_row, o0, o1)
            o_ref[0, pl.ds(pl.multiple_of((r0 + j) * GRID_W, GRID_W), GRID_W), :] = o.astype(o_ref.dtype)
        return carry

    lax.fori_loop(0, rows // NA_GROUP, group, 0, unroll=4)


def _na_attn(qkv3, tab):
    B, S, _ = qkv3.shape
    rows = S // GRID_W
    assert rows % NA_GROUP == 0 and rows >= NA_WIN
    n_pairs = NA_HEADS // 2
    blk = lambda c0: pl.BlockSpec((1, S, LANES), lambda hp, b: (b, 0, c0 + hp))
    return pl.pallas_call(
        functools.partial(_na_kernel, rows=rows),
        grid=(n_pairs, B),
        in_specs=[
            blk(0), blk(n_pairs), blk(2 * n_pairs),
            pl.BlockSpec((2, 3 * NA_PAIRS + 1, GRID_W, LANES), lambda hp, b: (hp, 0, 0, 0)),
        ],
        out_specs=pl.BlockSpec((1, S, LANES), lambda hp, b: (b, 0, hp)),
        out_shape=jax.ShapeDtypeStruct((B, S, NA_WIDTH), BF16),
        compiler_params=pltpu.CompilerParams(
            dimension_semantics=("parallel", "parallel"), vmem_limit_bytes=VMEM_LIMIT),
        name="na_attn",
    )(qkv3, qkv3, qkv3, tab)


def _hgrn_scales_step(row, col, n, reverse):
    C = row.shape[0]
    nv = C // SUBLANES
    if n < SUBLANES:
        r3 = row.reshape(nv, SUBLANES, LANES)
        c3 = col.reshape(nv, SUBLANES, LANES)
        sub = lax.broadcasted_iota(jnp.int32, (nv, SUBLANES, LANES), 1)
        in_g = (sub & n) != 0
        bc = lambda i: jnp.broadcast_to(r3[:, i:i + 1, :], r3.shape)
        if n == 1:
            up = pltpu.roll(r3, 1, axis=1)
            dn = pltpu.roll(r3, SUBLANES - 1, axis=1)
            if not reverse:
                new_r = jnp.where(in_g, r3 * up, r3)
                new_c = jnp.where(in_g, c3, c3 * dn)
            else:
                new_r = jnp.where(in_g, r3, r3 * dn)
                new_c = jnp.where(in_g, c3 * up, c3)
        else:
            lo = sub < 4
            if not reverse:
                if n == 2:
                    tf = jnp.where(lo, bc(1), bc(5))
                    tg = jnp.where(lo, bc(3), bc(7))
                else:
                    tf, tg = bc(3), bc(7)
            else:
                if n == 2:
                    tf = jnp.where(lo, bc(0), bc(4))
                    tg = jnp.where(lo, bc(2), bc(6))
                else:
                    tf, tg = bc(0), bc(4)
            if not reverse:
                new_r = jnp.where(in_g, r3 * tf, r3)
                new_c = jnp.where(in_g, c3, c3 * tg)
            else:
                new_r = jnp.where(in_g, r3, r3 * tg)
                new_c = jnp.where(in_g, c3 * tf, c3)
        return new_r.reshape(C, LANES), new_c.reshape(C, LANES)

    m = n // SUBLANES
    nb = nv // (2 * m)
    r5 = row.reshape(nb, 2, m, SUBLANES, LANES)
    c5 = col.reshape(nb, 2, m, SUBLANES, LANES)
    rf, rg = r5[:, 0], r5[:, 1]
    cf, cg = c5[:, 0], c5[:, 1]
    if not reverse:
        tf = rf[:, m - 1:m, SUBLANES - 1:SUBLANES, :]
        tg = rg[:, m - 1:m, SUBLANES - 1:SUBLANES, :]
        new_r = jnp.stack([rf, rg * tf], axis=1)
        new_c = jnp.stack([cf * tg, cg], axis=1)
    else:
        tf = rf[:, 0:1, 0:1, :]
        tg = rg[:, 0:1, 0:1, :]
        new_r = jnp.stack([rf * tg, rg], axis=1)
        new_c = jnp.stack([cf, cg * tf], axis=1)
    return new_r.reshape(C, LANES), new_c.reshape(C, LANES)


def _hgrn_direction(q, fpre, lb, v_bf, state_ref, level, reverse):
    C = q.shape[0]
    hc = C // 2
    halves = (slice(0, hc), slice(hc, C))
    sig = jax.nn.sigmoid(fpre)
    f = lb + (1.0 - lb) * sig
    k = (1.0 - lb) * jax.nn.sigmoid(-fpre)

    qb, kb = q.astype(BF16), k.astype(BF16)
    a_diag = [jnp.where(level == HG_LEVELS, _dot_nt(qb[h], kb[h]), 0.0) for h in halves]
    row, col = f, jnp.ones_like(f)
    n = 1
    for lv in range(HG_LEVELS):
        qs, ks = (q * row).astype(BF16), (k * col).astype(BF16)
        a_diag = [jnp.where(level == lv, _dot_nt(qs[h], ks[h]), a) for h, a in zip(halves, a_diag)]
        row, col = _hgrn_scales_step(row, col, n, reverse)
        n *= 2

    qs, ks = (q * row).astype(BF16), (k * col).astype(BF16)
    lo, hi = halves
    if not reverse:
        a_x = _dot_nt(qs[hi], ks[lo])
        o_lo = _dot(a_diag[0].astype(BF16), v_bf[lo])
        o_hi = _dot(jnp.concatenate([a_x, a_diag[1]], axis=1).astype(BF16), v_bf)
    else:
        a_x = _dot_nt(qs[lo], ks[hi])
        o_lo = _dot(jnp.concatenate([a_diag[0], a_x], axis=1).astype(BF16), v_bf)
        o_hi = _dot(a_diag[1].astype(BF16), v_bf[hi])
    row, col = _hgrn_scales_step(row, col, n, reverse)

    st = state_ref[...]
    o = jnp.concatenate([o_lo, o_hi], axis=0) + _dot_nt((q * row).astype(BF16), st.astype(BF16))
    total = row[C - 1:C, :] if not reverse else row[0:1, :]
    state_ref[...] = st * total + _dot_tn(v_bf, (k * col).astype(BF16))
    return o


def _hgrn_kernel(qf_ref, vf_ref, ff_ref, qb_ref, vb_ref, fb_ref, lbf_ref, lbb_ref, lvf_ref, lvb_ref,
                 of_ref, ob_ref, sf_ref, sb_ref):
    @pl.when(pl.program_id(2) == 0)
    def _():
        sf_ref[...] = jnp.zeros_like(sf_ref)
        sb_ref[...] = jnp.zeros_like(sb_ref)

    for h in range(HG_HPS):
        qf = jax.nn.silu(qf_ref[h])
        of_ref[h] = _hgrn_direction(qf, ff_ref[h], lbf_ref[h], vf_ref[h], sf_ref.at[h], lvf_ref[...], False)
        qb = jax.nn.silu(qb_ref[h])
        ob_ref[h] = _hgrn_direction(qb, fb_ref[h], lbb_ref[h], vb_ref[h], sb_ref.at[h], lvb_ref[...], True)


def _hgrn_level_tables():
    hc = HG_CHUNK // 2
    t = np.arange(hc)[:, None]
    s = np.arange(hc)[None, :]
    x = t ^ s
    hb = np.where(x > 0, np.floor(np.log2(np.maximum(x, 1))), HG_LEVELS).astype(np.int32)
    return jnp.asarray(np.where(t >= s, hb, -1)), jnp.asarray(np.where(t <= s, hb, -1))


def _hgrn(hq, hi, hff, hfb, lbf, lbb, batch):
    H, T, _ = hq.shape
    C = HG_CHUNK
    nc = T // batch // C
    fwd = pl.BlockSpec((HG_HPS, C, HG_DIM), lambda b, h, c: (h, b * nc + c, 0))
    bwd = pl.BlockSpec((HG_HPS, C, HG_DIM), lambda b, h, c: (h, b * nc + nc - 1 - c, 0))
    lb_spec = pl.BlockSpec((HG_HPS, 1, HG_DIM), lambda b, h, c: (h, 0, 0))
    lv_spec = pl.BlockSpec((C // 2, C // 2), lambda b, h, c: (0, 0))
    lvf, lvb = _hgrn_level_tables()
    out = jax.ShapeDtypeStruct((H, T, HG_DIM), F32)
    state = pltpu.VMEM((HG_HPS, HG_DIM, HG_DIM), F32)
    return pl.pallas_call(
        _hgrn_kernel,
        grid=(batch, H // HG_HPS, nc),
        in_specs=[fwd, fwd, fwd, bwd, bwd, bwd, lb_spec, lb_spec, lv_spec, lv_spec],
        out_specs=[fwd, bwd],
        out_shape=[out, out],
        scratch_shapes=[state, state],
        compiler_params=pltpu.CompilerParams(
            dimension_semantics=("parallel", "parallel", "arbitrary"), vmem_limit_bytes=VMEM_LIMIT),
        name="hgrn",
    )(hq, hi, hff, hq, hi, hfb, lbf, lbb, lvf, lvb)


def _mem_attn_kernel(q_ref, mem_ref, wkv_ref, o_ref, k_scr, v_scr):
    @pl.when(pl.program_id(1) == 0)
    def _():
        mb = mem_ref[0].astype(BF16)
        k_scr[...] = _dot(mb, wkv_ref[:, :MEM_WIDTH]).astype(BF16)
        v_scr[...] = _dot(mb, wkv_ref[:, MEM_WIDTH:]).astype(BF16)

    scale = MEM_HEAD_DIM ** -0.5
    for h in range(MEM_HEADS):
        sl = slice(h * MEM_HEAD_DIM, (h + 1) * MEM_HEAD_DIM)
        s = _dot_nt(q_ref[:, sl], k_scr[:, sl]) * scale
        m = jnp.max(s, axis=-1, keepdims=True)
        p = jnp.exp(s - m)
        l = jnp.sum(p, axis=-1, keepdims=True)
        o_ref[:, sl] = (_dot(p.astype(BF16), v_scr[:, sl]) / l).astype(o_ref.dtype)


def _mem_attn(mq, mem, wkv, seq, tm):
    B, M, _ = mem.shape
    nt = seq // tm
    return pl.pallas_call(
        _mem_attn_kernel,
        grid=(B, nt),
        in_specs=[
            pl.BlockSpec((tm, MEM_WIDTH), lambda b, i: (b * nt + i, 0)),
            pl.BlockSpec((1, M, D_MODEL), lambda b, i: (b, 0, 0)),
            pl.BlockSpec((D_MODEL, 2 * MEM_WIDTH), lambda b, i: (0, 0)),
        ],
        out_specs=pl.BlockSpec((tm, MEM_WIDTH), lambda b, i: (b * nt + i, 0)),
        out_shape=jax.ShapeDtypeStruct((B * seq, MEM_WIDTH), BF16),
        scratch_shapes=[pltpu.VMEM((M, MEM_WIDTH), BF16), pltpu.VMEM((M, MEM_WIDTH), BF16)],
        compiler_params=pltpu.CompilerParams(
            dimension_semantics=("parallel", "arbitrary"), vmem_limit_bytes=VMEM_LIMIT),
        name="mem_attn",
    )(mq, mem, wkv)


def _merge_kernel(x_ref, ge_ref, be_ref, yna_ref, of_ref, ob_ref, hog_ref, ng_ref, ymem_ref,
                  wg_ref, wna_ref, whg_ref, wmem_ref, wout_ref, g1_ref, b1_ref, o_ref):
    xn = _layer_norm(x_ref[...], ge_ref[...], be_ref[...])
    xb = xn.astype(BF16)

    heads = []
    for h in range(HG_HEADS):
        o = of_ref[h] + ob_ref[h]
        heads.append(o * lax.rsqrt(jnp.mean(o * o, axis=-1, keepdims=True) + RMS_EPS))
    y_hg = jnp.concatenate(heads, axis=-1) * ng_ref[...] * jax.nn.silu(hog_ref[...])

    def gate(j):
        return jax.nn.sigmoid(_dot(xb, wg_ref[:, j * D_MODEL:(j + 1) * D_MODEL]))

    merged = gate(0) * _dot(yna_ref[...], wna_ref[...])
    merged = merged + gate(1) * _dot(y_hg.astype(BF16), whg_ref[...])
    merged = merged + gate(2) * _dot(ymem_ref[...], wmem_ref[...])
    y = ALPHA * xn + _dot(merged.astype(BF16), wout_ref[...])
    o_ref[...] = _layer_norm(y, g1_ref[...], b1_ref[...])


def _merge(x2, ge, be, y_na, o_f, o_b, hog, ng, y_mem, wg, wna, whg, wmem, wout, g1, b1, tm):
    T = x2.shape[0]
    row = lambda n: pl.BlockSpec((tm, n), lambda i: (i, 0))
    vec = lambda n: pl.BlockSpec((1, n), lambda i: (0, 0))
    full = lambda a, b: pl.BlockSpec((a, b), lambda i: (0, 0))
    head = pl.BlockSpec((HG_HEADS, tm, HG_DIM), lambda i: (0, i, 0))
    return pl.pallas_call(
        _merge_kernel,
        grid=(T // tm,),
        in_specs=[
            row(D_MODEL), vec(D_MODEL), vec(D_MODEL),
            row(NA_WIDTH), head, head, row(HG_WIDTH), vec(HG_WIDTH), row(MEM_WIDTH),
            full(D_MODEL, 3 * D_MODEL), full(NA_WIDTH, D_MODEL), full(HG_WIDTH, D_MODEL),
            full(MEM_WIDTH, D_MODEL), full(D_MODEL, D_MODEL), vec(D_MODEL), vec(D_MODEL),
        ],
        out_specs=row(D_MODEL),
        out_shape=jax.ShapeDtypeStruct((T, D_MODEL), F32),
        compiler_params=pltpu.CompilerParams(
            dimension_semantics=("parallel",), vmem_limit_bytes=VMEM_LIMIT),
        name="merge",
    )(x2, ge, be, y_na, o_f, o_b, hog, ng, y_mem, wg, wna, whg, wmem, wout, g1, b1)


def _ffn_kernel(x_ref, w1_ref, w2_ref, g_ref, b_ref, o_ref):
    x = x_ref[...]
    xb = x.astype(BF16)
    acc = ALPHA * x
    for j in range(D_FF // D_MODEL):
        sl = slice(j * D_MODEL, (j + 1) * D_MODEL)
        h = jnp.maximum(_dot(xb, w1_ref[:, sl]), 0.0)
        acc = acc + _dot((h * h).astype(BF16), w2_ref[sl, :])
    o_ref[...] = _layer_norm(acc, g_ref[...], b_ref[...])


def _ffn(x1, w1, w2, g, b, tm):
    T = x1.shape[0]
    return pl.pallas_call(
        _ffn_kernel,
        grid=(T // tm,),
        in_specs=[
            pl.BlockSpec((tm, D_MODEL), lambda i: (i, 0)),
            pl.BlockSpec((D_MODEL, D_FF), lambda i: (0, 0)),
            pl.BlockSpec((D_FF, D_MODEL), lambda i: (0, 0)),
            pl.BlockSpec((1, D_MODEL), lambda i: (0, 0)),
            pl.BlockSpec((1, D_MODEL), lambda i: (0, 0)),
        ],
        out_specs=pl.BlockSpec((tm, D_MODEL), lambda i: (i, 0)),
        out_shape=jax.ShapeDtypeStruct((T, D_MODEL), F32),
        compiler_params=pltpu.CompilerParams(
            dimension_semantics=("parallel",), vmem_limit_bytes=VMEM_LIMIT),
        name="ffn",
    )(x1, w1, w2, g, b)


def kernel(x, mem, ln_emb_g, ln_emb_b, w_in, na_rpb, hg_lb_logits, hg_norm_g, w_mem_kv, w_branch_na,
           w_branch_hg, w_branch_mem, w_out, ln1_g, ln1_b, w_ff1, w_ff2, ln2_g, ln2_b):
    B, S, D = x.shape
    assert D == D_MODEL and S % HG_CHUNK == 0 and S % GRID_W == 0 and S // GRID_W >= NA_KH
    assert w_in.shape[0] == DEPTH
    T = B * S
    l = 0
    vec = lambda a: a.reshape(1, -1).astype(F32)

    w_in_b = w_in[l].astype(BF16)
    w_br = w_in_b[:, :BRANCH_COLS]
    w_gates = w_in_b[:, BRANCH_COLS:]
    lb_all = jnp.cumsum(jax.nn.softmax(hg_lb_logits.astype(F32), axis=1), axis=1)
    lbf = lb_all[0, l].reshape(HG_HEADS, 1, HG_DIM)
    lbb = lb_all[1, l].reshape(HG_HEADS, 1, HG_DIM)
    tab = _na_bias_table(na_rpb[l])

    x2 = x.reshape(T, D)
    ge, be = vec(ln_emb_g), vec(ln_emb_b)
    qkv, hq, hi, hog, hff, hfb, mq = _ln_proj(x2, ge, be, w_br, tm=512)
    y_na = _na_attn(qkv.reshape(B, S, 3 * NA_WIDTH), tab).reshape(T, NA_WIDTH)
    o_f, o_b = _hgrn(hq, hi, hff, hfb, lbf, lbb, B)
    y_mem = _mem_attn(mq, mem, w_mem_kv[l].astype(BF16), S, tm=512)
    x1 = _merge(x2, ge, be, y_na, o_f, o_b, hog, vec(hg_norm_g[l]), y_mem, w_gates,
                w_branch_na[l].astype(BF16), w_branch_hg[l].astype(BF16), w_branch_mem[l].astype(BF16),
                w_out[l].astype(BF16), vec(ln1_g[l]), vec(ln1_b[l]), tm=512)
    out = _ffn(x1, w_ff1[l].astype(BF16), w_ff2[l].astype(BF16), vec(ln2_g[l]), vec(ln2_b[l]), tm=512)
    return out.reshape(B, S, D)
```

```python
import functools

import numpy as np
import jax
import jax.numpy as jnp
from jax import lax
from jax.experimental import pallas as pl
from jax.experimental.pallas import tpu as pltpu

F32 = jnp.float32
BF16 = jnp.bfloat16

D_MODEL = 1024
GRID_W = 64
NA_HEADS = 8
NA_HEAD_DIM = 64
NA_WIDTH = NA_HEADS * NA_HEAD_DIM
NA_KH = 8
NA_KW = 16
HG_HEADS = 4
HG_DIM = 128
HG_WIDTH = HG_HEADS * HG_DIM
MEM_HEADS = 4
MEM_HEAD_DIM = 128
MEM_WIDTH = MEM_HEADS * MEM_HEAD_DIM
D_FF = 4 * D_MODEL
DEPTH = 1
ALPHA = (2.0 * DEPTH) ** 0.25
LN_EPS = 1e-5
RMS_EPS = 1e-6
BRANCH_COLS = 9 * 512
MASK_NEG = -1e30

LANES = 128
SUBLANES = 8
VMEM_LIMIT = 56 * 1024 * 1024
TOKEN_TILE = 1024
SUB_TILE = 256
HG_CHUNK = 256
HG_HPS = 4
HG_LEVELS = 7
NA_GROUP = 4
NA_WIN = 12
NA_PAIRS = 22
NA_UNROLL = 8


def _layer_norm(x, g, b):
    mu = jnp.mean(x, axis=-1, keepdims=True)
    xc = x - mu
    var = jnp.mean(xc * xc, axis=-1, keepdims=True)
    return xc * lax.rsqrt(var + LN_EPS) * g + b


def _sub_tiles(tm):
    return [slice(r, r + SUB_TILE) for r in range(0, tm, SUB_TILE)]


def _resident(shape):
    return pl.BlockSpec(shape, lambda *_: (0,) * len(shape), pipeline_mode=pl.Buffered(1))


def _dot(a, b):
    return jnp.dot(a, b, preferred_element_type=F32)


def _dot_nt(a, b):
    return lax.dot_general(a, b, (((1,), (1,)), ((), ())), preferred_element_type=F32)


def _dot_tn(a, b):
    return lax.dot_general(a, b, (((0,), (0,)), ((), ())), preferred_element_type=F32)


def _ln_proj_kernel(x_ref, g_ref, b_ref, w_ref, qkv_ref, hq_ref, hi_ref, hog_ref, hff_ref, hfb_ref, mq_ref):
    for rows in _sub_tiles(x_ref.shape[0]):
        xb = _layer_norm(x_ref[rows, :], g_ref[...], b_ref[...]).astype(BF16)

        def cols(j):
            return _dot(xb, w_ref[:, j * 512:(j + 1) * 512])

        for j in range(3):
            qkv_ref[rows, j * 512:(j + 1) * 512] = cols(j).astype(BF16)
        for j, ref in ((3, hq_ref), (4, hi_ref), (6, hff_ref), (7, hfb_ref)):
            r = cols(j)
            for h in range(HG_HEADS):
                ref[h, rows, :] = r[:, h * HG_DIM:(h + 1) * HG_DIM].astype(ref.dtype)
        hog_ref[rows, :] = cols(5)
        mq_ref[rows, :] = cols(8).astype(BF16)


def _ln_proj(x2, g, b, w_br):
    T = x2.shape[0]
    tm = TOKEN_TILE
    head_spec = pl.BlockSpec((HG_HEADS, tm, HG_DIM), lambda i: (0, i, 0))
    head_f32 = jax.ShapeDtypeStruct((HG_HEADS, T, HG_DIM), F32)
    return pl.pallas_call(
        _ln_proj_kernel,
        grid=(T // tm,),
        in_specs=[
            pl.BlockSpec((tm, D_MODEL), lambda i: (i, 0)),
            _resident((1, D_MODEL)), _resident((1, D_MODEL)), _resident((D_MODEL, BRANCH_COLS)),
        ],
        out_specs=[
            pl.BlockSpec((tm, 3 * NA_WIDTH), lambda i: (i, 0)),
            head_spec, head_spec,
            pl.BlockSpec((tm, HG_WIDTH), lambda i: (i, 0)),
            head_spec, head_spec,
            pl.BlockSpec((tm, MEM_WIDTH), lambda i: (i, 0)),
        ],
        out_shape=[
            jax.ShapeDtypeStruct((T, 3 * NA_WIDTH), BF16),
            head_f32,
            jax.ShapeDtypeStruct((HG_HEADS, T, HG_DIM), BF16),
            jax.ShapeDtypeStruct((T, HG_WIDTH), F32),
            head_f32, head_f32,
            jax.ShapeDtypeStruct((T, MEM_WIDTH), BF16),
        ],
        compiler_params=pltpu.CompilerParams(
            dimension_semantics=("parallel",), vmem_limit_bytes=VMEM_LIMIT),
        name="ln_proj",
    )(x2, g, b, w_br)


def _na_bias_table(rpb):
    qc = np.arange(GRID_W)[:, None]
    kc = np.arange(GRID_W)[None, :]
    c0 = np.clip(qc - NA_KW // 2, 0, GRID_W - NA_KW)
    valid = (kc >= c0) & (kc < c0 + NA_KW)
    dc = kc - qc + NA_KW - 1
    onehot = ((dc[None] == np.arange(2 * NA_KW - 1)[:, None, None]) & valid[None]).astype(np.float32)
    bnd = jnp.einsum("hdj,jqk->hdqk", rpb.astype(F32), jnp.asarray(onehot), precision=lax.Precision.HIGHEST)
    bnd = bnd + jnp.asarray(np.where(valid, 0.0, MASK_NEG).astype(np.float32))
    pad = jnp.full((NA_HEADS, 4, GRID_W, GRID_W), MASK_NEG, F32)
    ext = jnp.concatenate([pad, bnd, pad], axis=1)
    left, right = ext[:, :-1], ext[:, 1:]
    m = jnp.full_like(left, MASK_NEG)
    cat = lambda a, b: jnp.concatenate([a, b], axis=-1)
    return jnp.concatenate([cat(left, right), cat(left, m), cat(m, right), cat(m, m)[:, :1]], axis=1)


def _na_kernel(q_ref, k_ref, v_ref, tab_ref, o_ref, *, rows):
    scale = NA_HEAD_DIM ** -0.5
    assert np.log2(scale) == np.round(np.log2(scale))
    gq = NA_GROUP * GRID_W
    first = lax.broadcasted_iota(jnp.int32, (gq, LANES), 1) < NA_HEAD_DIM
    first_row = lax.broadcasted_iota(jnp.int32, (GRID_W, LANES), 1) < NA_HEAD_DIM

    def group(g, carry):
        r0 = g * NA_GROUP
        ws = jnp.clip(r0 - NA_KH // 2, 0, rows - NA_WIN)
        q = q_ref[0, pl.ds(pl.multiple_of(r0 * GRID_W, gq), gq), :] * jnp.asarray(scale, BF16)
        kw = k_ref[0, pl.ds(pl.multiple_of(ws * GRID_W, GRID_W), NA_WIN * GRID_W), :]
        vw = v_ref[0, pl.ds(pl.multiple_of(ws * GRID_W, GRID_W), NA_WIN * GRID_W), :]
        zero = jnp.zeros_like(q)
        qs = jnp.concatenate([jnp.where(first, q, zero), jnp.where(first, zero, q)], axis=0)
        s_all = _dot_nt(qs, kw)

        idx = []
        for j in range(NA_GROUP):
            r = r0 + j
            rs = jnp.clip(r - NA_KH // 2, 0, rows - NA_KH)
            row_idx = []
            for a2 in range(NA_WIN // 2):
                kr = ws + 2 * a2
                lv = jnp.logical_and(kr >= rs, kr < rs + NA_KH)
                rv = jnp.logical_and(kr + 1 >= rs, kr + 1 < rs + NA_KH)
                i = kr - r + (NA_KH - 1) + 4
                row_idx.append(jnp.where(jnp.logical_and(lv, rv), i,
                                         jnp.where(lv, NA_PAIRS + i,
                                                   jnp.where(rv, 2 * NA_PAIRS + i, 3 * NA_PAIRS))))
            idx.append(row_idx)

        ps, ls = [], []
        for h in range(2):
            for j in range(NA_GROUP):
                bias = jnp.concatenate([tab_ref[h, i] for i in idx[j]], axis=1)
                lo = (h * NA_GROUP + j) * GRID_W
                s = s_all[lo:lo + GRID_W] + bias
                m = jnp.max(s, axis=-1, keepdims=True)
                p = jnp.exp(s - m)
                ls.append(jnp.sum(p, axis=-1, keepdims=True))
                ps.append(p.astype(BF16))
        o_all = _dot(jnp.concatenate(ps, axis=0), vw)
        for j in range(NA_GROUP):
            o0 = o_all[j * GRID_W:(j + 1) * GRID_W] / ls[j]
            o1 = o_all[gq + j * GRID_W:gq + (j + 1) * GRID_W] / ls[NA_GROUP + j]
            o = jnp.where(first_row, o0, o1)
            o_ref[0, pl.ds(pl.multiple_of((r0 + j) * GRID_W, GRID_W), GRID_W), :] = o.astype(o_ref.dtype)
        return carry

    lax.fori_loop(0, rows // NA_GROUP, group, 0, unroll=NA_UNROLL)


def _na_attn(qkv3, tab):
    B, S, _ = qkv3.shape
    rows = S // GRID_W
    assert rows % NA_GROUP == 0 and rows >= NA_WIN
    n_pairs = NA_HEADS // 2
    blk = lambda c0: pl.BlockSpec((1, S, LANES), lambda hp, b: (b, 0, c0 + hp))
    return pl.pallas_call(
        functools.partial(_na_kernel, rows=rows),
        grid=(n_pairs, B),
        in_specs=[
            blk(0), blk(n_pairs), blk(2 * n_pairs),
            pl.BlockSpec((2, 3 * NA_PAIRS + 1, GRID_W, LANES), lambda hp, b: (hp, 0, 0, 0)),
        ],
        out_specs=pl.BlockSpec((1, S, LANES), lambda hp, b: (b, 0, hp)),
        out_shape=jax.ShapeDtypeStruct((B, S, NA_WIDTH), BF16),
        compiler_params=pltpu.CompilerParams(
            dimension_semantics=("parallel", "parallel"), vmem_limit_bytes=VMEM_LIMIT),
        name="na_attn",
    )(qkv3, qkv3, qkv3, tab)


def _hgrn_scales_step(row, col, n, reverse):
    C = row.shape[0]
    nv = C // SUBLANES
    if n < SUBLANES:
        r3 = row.reshape(nv, SUBLANES, LANES)
        c3 = col.reshape(nv, SUBLANES, LANES)
        sub = lax.broadcasted_iota(jnp.int32, (nv, SUBLANES, LANES), 1)
        in_g = (sub & n) != 0
        bc = lambda i: jnp.broadcast_to(r3[:, i:i + 1, :], r3.shape)
        if n == 1:
            up = pltpu.roll(r3, 1, axis=1)
            dn = pltpu.roll(r3, SUBLANES - 1, axis=1)
            if not reverse:
                new_r = jnp.where(in_g, r3 * up, r3)
                new_c = jnp.where(in_g, c3, c3 * dn)
            else:
                new_r = jnp.where(in_g, r3, r3 * dn)
                new_c = jnp.where(in_g, c3 * up, c3)
        else:
            lo = sub < 4
            if not reverse:
                if n == 2:
                    tf = jnp.where(lo, bc(1), bc(5))
                    tg = jnp.where(lo, bc(3), bc(7))
                else:
                    tf, tg = bc(3), bc(7)
            else:
                if n == 2:
                    tf = jnp.where(lo, bc(0), bc(4))
                    tg = jnp.where(lo, bc(2), bc(6))
                else:
                    tf, tg = bc(0), bc(4)
            if not reverse:
                new_r = jnp.where(in_g, r3 * tf, r3)
                new_c = jnp.where(in_g, c3, c3 * tg)
            else:
                new_r = jnp.where(in_g, r3, r3 * tg)
                new_c = jnp.where(in_g, c3 * tf, c3)
        return new_r.reshape(C, LANES), new_c.reshape(C, LANES)

    m = n // SUBLANES
    nb = nv // (2 * m)
    r5 = row.reshape(nb, 2, m, SUBLANES, LANES)
    c5 = col.reshape(nb, 2, m, SUBLANES, LANES)
    rf, rg = r5[:, 0], r5[:, 1]
    cf, cg = c5[:, 0], c5[:, 1]
    if not reverse:
        tf = rf[:, m - 1:m, SUBLANES - 1:SUBLANES, :]
        tg = rg[:, m - 1:m, SUBLANES - 1:SUBLANES, :]
        new_r = jnp.stack([rf, rg * tf], axis=1)
        new_c = jnp.stack([cf * tg, cg], axis=1)
    else:
        tf = rf[:, 0:1, 0:1, :]
        tg = rg[:, 0:1, 0:1, :]
        new_r = jnp.stack([rf * tg, rg], axis=1)
        new_c = jnp.stack([cf, cg * tf], axis=1)
    return new_r.reshape(C, LANES), new_c.reshape(C, LANES)


def _hgrn_direction(q, fpre, lb, v_bf, state_ref, level, reverse):
    C = q.shape[0]
    hc = C // 2
    halves = (slice(0, hc), slice(hc, C))
    sig = jax.nn.sigmoid(fpre)
    f = lb + (1.0 - lb) * sig
    k = (1.0 - lb) * jax.nn.sigmoid(-fpre)

    qb, kb = q.astype(BF16), k.astype(BF16)
    a_diag = [jnp.where(level == HG_LEVELS, _dot_nt(qb[h], kb[h]), 0.0) for h in halves]
    row, col = f, jnp.ones_like(f)
    n = 1
    for lv in range(HG_LEVELS):
        qs, ks = (q * row).astype(BF16), (k * col).astype(BF16)
        a_diag = [jnp.where(level == lv, _dot_nt(qs[h], ks[h]), a) for h, a in zip(halves, a_diag)]
        row, col = _hgrn_scales_step(row, col, n, reverse)
        n *= 2

    qs, ks = (q * row).astype(BF16), (k * col).astype(BF16)
    lo, hi = halves
    if not reverse:
        a_x = _dot_nt(qs[hi], ks[lo])
        o_lo = _dot(a_diag[0].astype(BF16), v_bf[lo])
        o_hi = _dot(jnp.concatenate([a_x, a_diag[1]], axis=1).astype(BF16), v_bf)
    else:
        a_x = _dot_nt(qs[lo], ks[hi])
        o_lo = _dot(jnp.concatenate([a_diag[0], a_x], axis=1).astype(BF16), v_bf)
        o_hi = _dot(a_diag[1].astype(BF16), v_bf[hi])
    row, col = _hgrn_scales_step(row, col, n, reverse)

    st = state_ref[...]
    o = jnp.concatenate([o_lo, o_hi], axis=0) + _dot_nt((q * row).astype(BF16), st.astype(BF16))
    total = row[C - 1:C, :] if not reverse else row[0:1, :]
    state_ref[...] = st * total + _dot_tn(v_bf, (k * col).astype(BF16))
    return o


def _hgrn_kernel(qf_ref, vf_ref, ff_ref, qb_ref, vb_ref, fb_ref, lbf_ref, lbb_ref, lvf_ref, lvb_ref,
                 of_ref, ob_ref, sf_ref, sb_ref):
    @pl.when(pl.program_id(2) == 0)
    def _():
        sf_ref[...] = jnp.zeros_like(sf_ref)
        sb_ref[...] = jnp.zeros_like(sb_ref)

    for h in range(HG_HPS):
        qf = jax.nn.silu(qf_ref[h])
        of_ref[h] = _hgrn_direction(qf, ff_ref[h], lbf_ref[h], vf_ref[h], sf_ref.at[h], lvf_ref[...], False)
        qb = jax.nn.silu(qb_ref[h])
        ob_ref[h] = _hgrn_direction(qb, fb_ref[h], lbb_ref[h], vb_ref[h], sb_ref.at[h], lvb_ref[...], True)


def _hgrn_level_tables():
    hc = HG_CHUNK // 2
    t = np.arange(hc)[:, None]
    s = np.arange(hc)[None, :]
    x = t ^ s
    hb = np.where(x > 0, np.floor(np.log2(np.maximum(x, 1))), HG_LEVELS).astype(np.int32)
    return jnp.asarray(np.where(t >= s, hb, -1)), jnp.asarray(np.where(t <= s, hb, -1))


def _hgrn(hq, hi, hff, hfb, lbf, lbb, batch):
    H, T, _ = hq.shape
    C = HG_CHUNK
    nc = T // batch // C
    fwd = pl.BlockSpec((HG_HPS, C, HG_DIM), lambda b, h, c: (h, b * nc + c, 0))
    bwd = pl.BlockSpec((HG_HPS, C, HG_DIM), lambda b, h, c: (h, b * nc + nc - 1 - c, 0))
    lb_spec = pl.BlockSpec((HG_HPS, 1, HG_DIM), lambda b, h, c: (h, 0, 0))
    lv_spec = _resident((C // 2, C // 2))
    lvf, lvb = _hgrn_level_tables()
    out = jax.ShapeDtypeStruct((H, T, HG_DIM), F32)
    state = pltpu.VMEM((HG_HPS, HG_DIM, HG_DIM), F32)
    return pl.pallas_call(
        _hgrn_kernel,
        grid=(batch, H // HG_HPS, nc),
        in_specs=[fwd, fwd, fwd, bwd, bwd, bwd, lb_spec, lb_spec, lv_spec, lv_spec],
        out_specs=[fwd, bwd],
        out_shape=[out, out],
        scratch_shapes=[state, state],
        compiler_params=pltpu.CompilerParams(
            dimension_semantics=("parallel", "parallel", "arbitrary"), vmem_limit_bytes=VMEM_LIMIT),
        name="hgrn",
    )(hq, hi, hff, hq, hi, hfb, lbf, lbb, lvf, lvb)


def _mem_attn_kernel(q_ref, mem_ref, wkv_ref, o_ref, k_scr, v_scr):
    @pl.when(pl.program_id(1) == 0)
    def _():
        mb = mem_ref[0].astype(BF16)
        k_scr[...] = _dot(mb, wkv_ref[:, :MEM_WIDTH]).astype(BF16)
        v_scr[...] = _dot(mb, wkv_ref[:, MEM_WIDTH:]).astype(BF16)

    scale = MEM_HEAD_DIM ** -0.5
    for h in range(MEM_HEADS):
        sl = slice(h * MEM_HEAD_DIM, (h + 1) * MEM_HEAD_DIM)
        s = _dot_nt(q_ref[:, sl], k_scr[:, sl]) * scale
        m = jnp.max(s, axis=-1, keepdims=True)
        p = jnp.exp(s - m)
        l = jnp.sum(p, axis=-1, keepdims=True)
        o_ref[:, sl] = (_dot(p.astype(BF16), v_scr[:, sl]) / l).astype(o_ref.dtype)


def _mem_attn(mq, mem, wkv, seq, tm):
    B, M, _ = mem.shape
    nt = seq // tm
    return pl.pallas_call(
        _mem_attn_kernel,
        grid=(B, nt),
        in_specs=[
            pl.BlockSpec((tm, MEM_WIDTH), lambda b, i: (b * nt + i, 0)),
            pl.BlockSpec((1, M, D_MODEL), lambda b, i: (b, 0, 0)),
            _resident((D_MODEL, 2 * MEM_WIDTH)),
        ],
        out_specs=pl.BlockSpec((tm, MEM_WIDTH), lambda b, i: (b * nt + i, 0)),
        out_shape=jax.ShapeDtypeStruct((B * seq, MEM_WIDTH), BF16),
        scratch_shapes=[pltpu.VMEM((M, MEM_WIDTH), BF16), pltpu.VMEM((M, MEM_WIDTH), BF16)],
        compiler_params=pltpu.CompilerParams(
            dimension_semantics=("parallel", "arbitrary"), vmem_limit_bytes=VMEM_LIMIT),
        name="mem_attn",
    )(mq, mem, wkv)


def _merge_kernel(x_ref, ge_ref, be_ref, yna_ref, of_ref, ob_ref, hog_ref, ng_ref, ymem_ref,
                  wg_ref, wna_ref, whg_ref, wmem_ref, wout_ref, g1_ref, b1_ref, o_ref):
    for rows in _sub_tiles(x_ref.shape[0]):
        xn = _layer_norm(x_ref[rows, :], ge_ref[...], be_ref[...])
        xb = xn.astype(BF16)

        heads = []
        for h in range(HG_HEADS):
            o = of_ref[h, rows, :] + ob_ref[h, rows, :]
            heads.append(o * lax.rsqrt(jnp.mean(o * o, axis=-1, keepdims=True) + RMS_EPS))
        y_hg = jnp.concatenate(heads, axis=-1) * ng_ref[...] * jax.nn.silu(hog_ref[rows, :])

        def gate(j):
            return jax.nn.sigmoid(_dot(xb, wg_ref[:, j * D_MODEL:(j + 1) * D_MODEL]))

        merged = gate(0) * _dot(yna_ref[rows, :], wna_ref[...])
        merged = merged + gate(1) * _dot(y_hg.astype(BF16), whg_ref[...])
        merged = merged + gate(2) * _dot(ymem_ref[rows, :], wmem_ref[...])
        y = ALPHA * xn + _dot(merged.astype(BF16), wout_ref[...])
        o_ref[rows, :] = _layer_norm(y, g1_ref[...], b1_ref[...])


def _merge(x2, ge, be, y_na, o_f, o_b, hog, ng, y_mem, wg, wna, whg, wmem, wout, g1, b1):
    T = x2.shape[0]
    tm = TOKEN_TILE
    row = lambda n: pl.BlockSpec((tm, n), lambda i: (i, 0))
    vec = lambda n: _resident((1, n))
    head = pl.BlockSpec((HG_HEADS, tm, HG_DIM), lambda i: (0, i, 0))
    return pl.pallas_call(
        _merge_kernel,
        grid=(T // tm,),
        in_specs=[
            row(D_MODEL), vec(D_MODEL), vec(D_MODEL),
            row(NA_WIDTH), head, head, row(HG_WIDTH), vec(HG_WIDTH), row(MEM_WIDTH),
            _resident((D_MODEL, 3 * D_MODEL)), _resident((NA_WIDTH, D_MODEL)), _resident((HG_WIDTH, D_MODEL)),
            _resident((MEM_WIDTH, D_MODEL)), _resident((D_MODEL, D_MODEL)), vec(D_MODEL), vec(D_MODEL),
        ],
        out_specs=row(D_MODEL),
        out_shape=jax.ShapeDtypeStruct((T, D_MODEL), F32),
        compiler_params=pltpu.CompilerParams(
            dimension_semantics=("parallel",), vmem_limit_bytes=VMEM_LIMIT),
        name="merge",
    )(x2, ge, be, y_na, o_f, o_b, hog, ng, y_mem, wg, wna, whg, wmem, wout, g1, b1)


def _ffn_kernel(x_ref, w1_ref, w2_ref, g_ref, b_ref, o_ref):
    for rows in _sub_tiles(x_ref.shape[0]):
        x = x_ref[rows, :]
        xb = x.astype(BF16)
        acc = ALPHA * x
        for j in range(D_FF // D_MODEL):
            sl = slice(j * D_MODEL, (j + 1) * D_MODEL)
            h = jnp.maximum(_dot(xb, w1_ref[:, sl]), 0.0)
            acc = acc + _dot((h * h).astype(BF16), w2_ref[sl, :])
        o_ref[rows, :] = _layer_norm(acc, g_ref[...], b_ref[...])


def _ffn(x1, w1, w2, g, b):
    T = x1.shape[0]
    tm = TOKEN_TILE
    return pl.pallas_call(
        _ffn_kernel,
        grid=(T // tm,),
        in_specs=[
            pl.BlockSpec((tm, D_MODEL), lambda i: (i, 0)),
            _resident((D_MODEL, D_FF)), _resident((D_FF, D_MODEL)),
            _resident((1, D_MODEL)), _resident((1, D_MODEL)),
        ],
        out_specs=pl.BlockSpec((tm, D_MODEL), lambda i: (i, 0)),
        out_shape=jax.ShapeDtypeStruct((T, D_MODEL), F32),
        compiler_params=pltpu.CompilerParams(
            dimension_semantics=("parallel",), vmem_limit_bytes=VMEM_LIMIT),
        name="ffn",
    )(x1, w1, w2, g, b)


def kernel(x, mem, ln_emb_g, ln_emb_b, w_in, na_rpb, hg_lb_logits, hg_norm_g, w_mem_kv, w_branch_na,
           w_branch_hg, w_branch_mem, w_out, ln1_g, ln1_b, w_ff1, w_ff2, ln2_g, ln2_b):
    B, S, D = x.shape
    assert D == D_MODEL and S % HG_CHUNK == 0 and S % GRID_W == 0 and S // GRID_W >= NA_KH
    assert w_in.shape[0] == DEPTH and (B * S) % TOKEN_TILE == 0
    T = B * S
    l = 0
    vec = lambda a: a.reshape(1, -1).astype(F32)

    w_br = w_in[l, :, :BRANCH_COLS].astype(BF16)
    w_gates = w_in[l, :, BRANCH_COLS:].astype(BF16)
    lb_all = jnp.cumsum(jax.nn.softmax(hg_lb_logits.astype(F32), axis=1), axis=1)
    lbf = lb_all[0, l].reshape(HG_HEADS, 1, HG_DIM)
    lbb = lb_all[1, l].reshape(HG_HEADS, 1, HG_DIM)
    tab = _na_bias_table(na_rpb[l])

    x2 = x.reshape(T, D)
    ge, be = vec(ln_emb_g), vec(ln_emb_b)
    qkv, hq, hi, hog, hff, hfb, mq = _ln_proj(x2, ge, be, w_br)
    y_na = _na_attn(qkv.reshape(B, S, 3 * NA_WIDTH), tab).reshape(T, NA_WIDTH)
    o_f, o_b = _hgrn(hq, hi, hff, hfb, lbf, lbb, B)
    y_mem = _mem_attn(mq, mem, w_mem_kv[l].astype(BF16), S, tm=512)
    x1 = _merge(x2, ge, be, y_na, o_f, o_b, hog, vec(hg_norm_g[l]), y_mem, w_gates,
                w_branch_na[l].astype(BF16), w_branch_hg[l].astype(BF16), w_branch_mem[l].astype(BF16),
                w_out[l].astype(BF16), vec(ln1_g[l]), vec(ln1_b[l]))
    out = _ffn(x1, w_ff1[l].astype(BF16), w_ff2[l].astype(BF16), vec(ln2_g[l]), vec(ln2_b[l]))
    return out.reshape(B, S, D)
```

```python
import functools

import numpy as np
import jax
import jax.numpy as jnp
from jax import lax
from jax.experimental import pallas as pl
from jax.experimental.pallas import tpu as pltpu

F32 = jnp.float32
BF16 = jnp.bfloat16

D_MODEL = 1024
GRID_W = 64
NA_HEADS = 8
NA_HEAD_DIM = 64
NA_WIDTH = NA_HEADS * NA_HEAD_DIM
NA_KH = 8
NA_KW = 16
HG_HEADS = 4
HG_DIM = 128
HG_WIDTH = HG_HEADS * HG_DIM
MEM_HEADS = 4
MEM_HEAD_DIM = 128
MEM_WIDTH = MEM_HEADS * MEM_HEAD_DIM
D_FF = 4 * D_MODEL
DEPTH = 1
ALPHA = (2.0 * DEPTH) ** 0.25
LN_EPS = 1e-5
RMS_EPS = 1e-6
BRANCH_COLS = 9 * 512
W_IN_BLOCK = 1536
MASK_NEG = -1e30

LANES = 128
SUBLANES = 8
VMEM_LIMIT = 56 * 1024 * 1024
TOKEN_TILE = 1024
SUB_TILE = 256
HG_CHUNK = 256
HG_HPS = 4
HG_LEVELS = 7
NA_GROUP = 4
NA_WIN = 12
NA_PAIRS = 22
NA_UNROLL = 8


def _layer_norm(x, g, b):
    mu = jnp.mean(x, axis=-1, keepdims=True)
    xc = x - mu
    var = jnp.mean(xc * xc, axis=-1, keepdims=True)
    return xc * lax.rsqrt(var + LN_EPS) * g + b


def _sub_tiles(tm):
    return [slice(r, r + SUB_TILE) for r in range(0, tm, SUB_TILE)]


def _resident(shape):
    return pl.BlockSpec(shape, lambda *_: (0,) * len(shape), pipeline_mode=pl.Buffered(1))


def _dot(a, b):
    return jnp.dot(a, b, preferred_element_type=F32)


def _dot_nt(a, b):
    return lax.dot_general(a, b, (((1,), (1,)), ((), ())), preferred_element_type=F32)


def _dot_tn(a, b):
    return lax.dot_general(a, b, (((0,), (0,)), ((), ())), preferred_element_type=F32)


def _w_in_cols(w_refs, first_block, c0, n):
    blk, off = divmod(c0 - first_block * W_IN_BLOCK, W_IN_BLOCK)
    assert off + n <= W_IN_BLOCK
    return w_refs[blk][:, off:off + n]


def _ln_proj_kernel(x_ref, g_ref, b_ref, w0_ref, w1_ref, w2_ref, mem_ref, wkv_ref,
                    qkv_ref, hq_ref, hi_ref, hog_ref, hff_ref, hfb_ref, ymem_ref, k_scr, v_scr, *, tiles_per_seq):
    @pl.when(pl.program_id(0) % tiles_per_seq == 0)
    def _():
        mb = mem_ref[0].astype(BF16)
        k_scr[...] = _dot(mb, wkv_ref[:, :MEM_WIDTH]).astype(BF16)
        v_scr[...] = _dot(mb, wkv_ref[:, MEM_WIDTH:]).astype(BF16)

    w_refs = (w0_ref, w1_ref, w2_ref)
    mem_scale = MEM_HEAD_DIM ** -0.5
    for rows in _sub_tiles(x_ref.shape[0]):
        xb = _layer_norm(x_ref[rows, :], g_ref[...], b_ref[...]).astype(BF16)

        def cols(j):
            return _dot(xb, _w_in_cols(w_refs, 0, j * 512, 512))

        for j in range(3):
            qkv_ref[rows, j * 512:(j + 1) * 512] = cols(j).astype(BF16)
        for j, ref in ((3, hq_ref), (4, hi_ref), (6, hff_ref), (7, hfb_ref)):
            r = cols(j)
            for h in range(HG_HEADS):
                ref[h, rows, :] = r[:, h * HG_DIM:(h + 1) * HG_DIM].astype(ref.dtype)
        hog_ref[rows, :] = cols(5)

        mq = cols(8).astype(BF16)
        for h in range(MEM_HEADS):
            sl = slice(h * MEM_HEAD_DIM, (h + 1) * MEM_HEAD_DIM)
            s = _dot_nt(mq[:, sl], k_scr[:, sl]) * mem_scale
            m = jnp.max(s, axis=-1, keepdims=True)
            p = jnp.exp(s - m)
            l = jnp.sum(p, axis=-1, keepdims=True)
            ymem_ref[rows, sl] = (_dot(p.astype(BF16), v_scr[:, sl]) / l).astype(ymem_ref.dtype)


def _ln_proj(x2, g, b, w_in_b, mem, wkv, seq):
    T = x2.shape[0]
    tm = TOKEN_TILE
    assert seq % tm == 0 and BRANCH_COLS == 3 * W_IN_BLOCK
    tiles_per_seq = seq // tm
    M = mem.shape[1]
    head_spec = pl.BlockSpec((HG_HEADS, tm, HG_DIM), lambda i: (0, i, 0))
    head_f32 = jax.ShapeDtypeStruct((HG_HEADS, T, HG_DIM), F32)
    w_blk = lambda j: pl.BlockSpec((D_MODEL, W_IN_BLOCK), lambda i: (0, j), pipeline_mode=pl.Buffered(1))
    return pl.pallas_call(
        functools.partial(_ln_proj_kernel, tiles_per_seq=tiles_per_seq),
        grid=(T // tm,),
        in_specs=[
            pl.BlockSpec((tm, D_MODEL), lambda i: (i, 0)),
            _resident((1, D_MODEL)), _resident((1, D_MODEL)), w_blk(0), w_blk(1), w_blk(2),
            pl.BlockSpec((1, M, D_MODEL), lambda i: (i // tiles_per_seq, 0, 0)),
            _resident((D_MODEL, 2 * MEM_WIDTH)),
        ],
        out_specs=[
            pl.BlockSpec((tm, 3 * NA_WIDTH), lambda i: (i, 0)),
            head_spec, head_spec,
            pl.BlockSpec((tm, HG_WIDTH), lambda i: (i, 0)),
            head_spec, head_spec,
            pl.BlockSpec((tm, MEM_WIDTH), lambda i: (i, 0)),
        ],
        out_shape=[
            jax.ShapeDtypeStruct((T, 3 * NA_WIDTH), BF16),
            head_f32,
            jax.ShapeDtypeStruct((HG_HEADS, T, HG_DIM), BF16),
            jax.ShapeDtypeStruct((T, HG_WIDTH), F32),
            head_f32, head_f32,
            jax.ShapeDtypeStruct((T, MEM_WIDTH), BF16),
        ],
        scratch_shapes=[pltpu.VMEM((M, MEM_WIDTH), BF16), pltpu.VMEM((M, MEM_WIDTH), BF16)],
        compiler_params=pltpu.CompilerParams(
            dimension_semantics=("arbitrary",), vmem_limit_bytes=VMEM_LIMIT),
        name="ln_proj",
    )(x2, g, b, w_in_b, w_in_b, w_in_b, mem, wkv)


def _na_bias_bands(rpb):
    qc = np.arange(GRID_W)[:, None]
    kc = np.arange(GRID_W)[None, :]
    c0 = np.clip(qc - NA_KW // 2, 0, GRID_W - NA_KW)
    valid = (kc >= c0) & (kc < c0 + NA_KW)
    dc = kc - qc + NA_KW - 1
    onehot = ((dc[None] == np.arange(2 * NA_KW - 1)[:, None, None]) & valid[None]).astype(np.float32)
    bnd = jnp.einsum("hdj,jqk->hdqk", rpb.astype(F32), jnp.asarray(onehot), precision=lax.Precision.HIGHEST)
    bnd = bnd + jnp.asarray(np.where(valid, 0.0, MASK_NEG).astype(np.float32))
    pad = jnp.full((NA_HEADS, 4, GRID_W, GRID_W), MASK_NEG, F32)
    return jnp.concatenate([pad, bnd, pad], axis=1)


def _na_kernel(q_ref, k_ref, v_ref, band_ref, o_ref, tab_ref, *, rows):
    @pl.when(pl.program_id(1) == 0)
    def _():
        masked = jnp.full((GRID_W, GRID_W), MASK_NEG, F32)
        cat = lambda a, b: jnp.concatenate([a, b], axis=-1)
        for h in range(2):
            for i in range(NA_PAIRS):
                left, right = band_ref[h, i], band_ref[h, i + 1]
                tab_ref[h, i] = cat(left, right)
                tab_ref[h, NA_PAIRS + i] = cat(left, masked)
                tab_ref[h, 2 * NA_PAIRS + i] = cat(masked, right)
            tab_ref[h, 3 * NA_PAIRS] = cat(masked, masked)

    scale = NA_HEAD_DIM ** -0.5
    assert np.log2(scale) == np.round(np.log2(scale))
    gq = NA_GROUP * GRID_W
    first = lax.broadcasted_iota(jnp.int32, (gq, LANES), 1) < NA_HEAD_DIM
    first_row = lax.broadcasted_iota(jnp.int32, (GRID_W, LANES), 1) < NA_HEAD_DIM

    def group(g, carry):
        r0 = g * NA_GROUP
        ws = jnp.clip(r0 - NA_KH // 2, 0, rows - NA_WIN)
        q = q_ref[0, pl.ds(pl.multiple_of(r0 * GRID_W, gq), gq), :] * jnp.asarray(scale, BF16)
        kw = k_ref[0, pl.ds(pl.multiple_of(ws * GRID_W, GRID_W), NA_WIN * GRID_W), :]
        vw = v_ref[0, pl.ds(pl.multiple_of(ws * GRID_W, GRID_W), NA_WIN * GRID_W), :]
        zero = jnp.zeros_like(q)
        qs = jnp.concatenate([jnp.where(first, q, zero), jnp.where(first, zero, q)], axis=0)
        s_all = _dot_nt(qs, kw)

        idx = []
        for j in range(NA_GROUP):
            r = r0 + j
            rs = jnp.clip(r - NA_KH // 2, 0, rows - NA_KH)
            row_idx = []
            for a2 in range(NA_WIN // 2):
                kr = ws + 2 * a2
                lv = jnp.logical_and(kr >= rs, kr < rs + NA_KH)
                rv = jnp.logical_and(kr + 1 >= rs, kr + 1 < rs + NA_KH)
                i = kr - r + (NA_KH - 1) + 4
                row_idx.append(jnp.where(jnp.logical_and(lv, rv), i,
                                         jnp.where(lv, NA_PAIRS + i,
                                                   jnp.where(rv, 2 * NA_PAIRS + i, 3 * NA_PAIRS))))
            idx.append(row_idx)

        ps, ls = [], []
        for h in range(2):
            for j in range(NA_GROUP):
                bias = jnp.concatenate([tab_ref[h, i] for i in idx[j]], axis=1)
                lo = (h * NA_GROUP + j) * GRID_W
                s = s_all[lo:lo + GRID_W] + bias
                m = jnp.max(s, axis=-1, keepdims=True)
                p = jnp.exp(s - m)
                ls.append(jnp.sum(p, axis=-1, keepdims=True))
                ps.append(p.astype(BF16))
        o_all = _dot(jnp.concatenate(ps, axis=0), vw)
        for j in range(NA_GROUP):
            o0 = o_all[j * GRID_W:(j + 1) * GRID_W] / ls[j]
            o1 = o_all[gq + j * GRID_W:gq + (j + 1) * GRID_W] / ls[NA_GROUP + j]
            o = jnp.where(first_row, o0, o1)
            o_ref[0, pl.ds(pl.multiple_of((r0 + j) * GRID_W, GRID_W), GRID_W), :] = o.astype(o_ref.dtype)
        return carry

    lax.fori_loop(0, rows // NA_GROUP, group, 0, unroll=NA_UNROLL)


def _na_attn(qkv3, bands):
    B, S, _ = qkv3.shape
    rows = S // GRID_W
    assert rows % NA_GROUP == 0 and rows >= NA_WIN and bands.shape[1] == NA_PAIRS + 1
    n_pairs = NA_HEADS // 2
    blk = lambda c0: pl.BlockSpec((1, S, LANES), lambda hp, b: (b, 0, c0 + hp))
    return pl.pallas_call(
        functools.partial(_na_kernel, rows=rows),
        grid=(n_pairs, B),
        in_specs=[
            blk(0), blk(n_pairs), blk(2 * n_pairs),
            pl.BlockSpec((2, NA_PAIRS + 1, GRID_W, GRID_W), lambda hp, b: (hp, 0, 0, 0)),
        ],
        out_specs=pl.BlockSpec((1, S, LANES), lambda hp, b: (b, 0, hp)),
        out_shape=jax.ShapeDtypeStruct((B, S, NA_WIDTH), BF16),
        scratch_shapes=[pltpu.VMEM((2, 3 * NA_PAIRS + 1, GRID_W, LANES), F32)],
        compiler_params=pltpu.CompilerParams(
            dimension_semantics=("arbitrary", "arbitrary"), vmem_limit_bytes=VMEM_LIMIT),
        name="na_attn",
    )(qkv3, qkv3, qkv3, bands)


def _hgrn_scales_step(row, col, n, reverse):
    C = row.shape[0]
    nv = C // SUBLANES
    if n < SUBLANES:
        r3 = row.reshape(nv, SUBLANES, LANES)
        c3 = col.reshape(nv, SUBLANES, LANES)
        sub = lax.broadcasted_iota(jnp.int32, (nv, SUBLANES, LANES), 1)
        in_g = (sub & n) != 0
        bc = lambda i: jnp.broadcast_to(r3[:, i:i + 1, :], r3.shape)
        if n == 1:
            up = pltpu.roll(r3, 1, axis=1)
            dn = pltpu.roll(r3, SUBLANES - 1, axis=1)
            if not reverse:
                new_r = jnp.where(in_g, r3 * up, r3)
                new_c = jnp.where(in_g, c3, c3 * dn)
            else:
                new_r = jnp.where(in_g, r3, r3 * dn)
                new_c = jnp.where(in_g, c3 * up, c3)
        else:
            lo = sub < 4
            if not reverse:
                if n == 2:
                    tf = jnp.where(lo, bc(1), bc(5))
                    tg = jnp.where(lo, bc(3), bc(7))
                else:
                    tf, tg = bc(3), bc(7)
            else:
                if n == 2:
                    tf = jnp.where(lo, bc(0), bc(4))
                    tg = jnp.where(lo, bc(2), bc(6))
                else:
                    tf, tg = bc(0), bc(4)
            if not reverse:
                new_r = jnp.where(in_g, r3 * tf, r3)
                new_c = jnp.where(in_g, c3, c3 * tg)
            else:
                new_r = jnp.where(in_g, r3, r3 * tg)
                new_c = jnp.where(in_g, c3 * tf, c3)
        return new_r.reshape(C, LANES), new_c.reshape(C, LANES)

    m = n // SUBLANES
    nb = nv // (2 * m)
    r5 = row.reshape(nb, 2, m, SUBLANES, LANES)
    c5 = col.reshape(nb, 2, m, SUBLANES, LANES)
    rf, rg = r5[:, 0], r5[:, 1]
    cf, cg = c5[:, 0], c5[:, 1]
    if not reverse:
        tf = rf[:, m - 1:m, SUBLANES - 1:SUBLANES, :]
        tg = rg[:, m - 1:m, SUBLANES - 1:SUBLANES, :]
        new_r = jnp.stack([rf, rg * tf], axis=1)
        new_c = jnp.stack([cf * tg, cg], axis=1)
    else:
        tf = rf[:, 0:1, 0:1, :]
        tg = rg[:, 0:1, 0:1, :]
        new_r = jnp.stack([rf * tg, rg], axis=1)
        new_c = jnp.stack([cf, cg * tf], axis=1)
    return new_r.reshape(C, LANES), new_c.reshape(C, LANES)


def _hgrn_direction(q, fpre, lb, v_bf, state_ref, level, reverse):
    C = q.shape[0]
    hc = C // 2
    halves = (slice(0, hc), slice(hc, C))
    sig = jax.nn.sigmoid(fpre)
    f = lb + (1.0 - lb) * sig
    k = (1.0 - lb) * jax.nn.sigmoid(-fpre)

    qb, kb = q.astype(BF16), k.astype(BF16)
    a_diag = [jnp.where(level == HG_LEVELS, _dot_nt(qb[h], kb[h]), 0.0) for h in halves]
    row, col = f, jnp.ones_like(f)
    n = 1
    for lv in range(HG_LEVELS):
        qs, ks = (q * row).astype(BF16), (k * col).astype(BF16)
        a_diag = [jnp.where(level == lv, _dot_nt(qs[h], ks[h]), a) for h, a in zip(halves, a_diag)]
        row, col = _hgrn_scales_step(row, col, n, reverse)
        n *= 2

    qs, ks = (q * row).astype(BF16), (k * col).astype(BF16)
    lo, hi = halves
    if not reverse:
        a_x = _dot_nt(qs[hi], ks[lo])
        o_lo = _dot(a_diag[0].astype(BF16), v_bf[lo])
        o_hi = _dot(jnp.concatenate([a_x, a_diag[1]], axis=1).astype(BF16), v_bf)
    else:
        a_x = _dot_nt(qs[lo], ks[hi])
        o_lo = _dot(jnp.concatenate([a_diag[0], a_x], axis=1).astype(BF16), v_bf)
        o_hi = _dot(a_diag[1].astype(BF16), v_bf[hi])
    row, col = _hgrn_scales_step(row, col, n, reverse)

    st = state_ref[...]
    o = jnp.concatenate([o_lo, o_hi], axis=0) + _dot_nt((q * row).astype(BF16), st.astype(BF16))
    total = row[C - 1:C, :] if not reverse else row[0:1, :]
    state_ref[...] = st * total + _dot_tn(v_bf, (k * col).astype(BF16))
    return o


def _hgrn_kernel(qf_ref, vf_ref, ff_ref, qb_ref, vb_ref, fb_ref, lbf_ref, lbb_ref, lvf_ref, lvb_ref,
                 of_ref, ob_ref, sf_ref, sb_ref):
    @pl.when(pl.program_id(2) == 0)
    def _():
        sf_ref[...] = jnp.zeros_like(sf_ref)
        sb_ref[...] = jnp.zeros_like(sb_ref)

    for h in range(HG_HPS):
        qf = jax.nn.silu(qf_ref[h])
        of_ref[h] = _hgrn_direction(qf, ff_ref[h], lbf_ref[h], vf_ref[h], sf_ref.at[h], lvf_ref[...], False)
        qb = jax.nn.silu(qb_ref[h])
        ob_ref[h] = _hgrn_direction(qb, fb_ref[h], lbb_ref[h], vb_ref[h], sb_ref.at[h], lvb_ref[...], True)


def _hgrn_level_tables():
    hc = HG_CHUNK // 2
    t = np.arange(hc)[:, None]
    s = np.arange(hc)[None, :]
    x = t ^ s
    hb = np.where(x > 0, np.floor(np.log2(np.maximum(x, 1))), HG_LEVELS).astype(np.int32)
    return jnp.asarray(np.where(t >= s, hb, -1)), jnp.asarray(np.where(t <= s, hb, -1))


def _hgrn(hq, hi, hff, hfb, lbf, lbb, batch):
    H, T, _ = hq.shape
    C = HG_CHUNK
    nc = T // batch // C
    fwd = pl.BlockSpec((HG_HPS, C, HG_DIM), lambda b, h, c: (h, b * nc + c, 0))
    bwd = pl.BlockSpec((HG_HPS, C, HG_DIM), lambda b, h, c: (h, b * nc + nc - 1 - c, 0))
    lb_spec = pl.BlockSpec((HG_HPS, 1, HG_DIM), lambda b, h, c: (h, 0, 0))
    lv_spec = _resident((C // 2, C // 2))
    lvf, lvb = _hgrn_level_tables()
    out = jax.ShapeDtypeStruct((H, T, HG_DIM), F32)
    state = pltpu.VMEM((HG_HPS, HG_DIM, HG_DIM), F32)
    return pl.pallas_call(
        _hgrn_kernel,
        grid=(batch, H // HG_HPS, nc),
        in_specs=[fwd, fwd, fwd, bwd, bwd, bwd, lb_spec, lb_spec, lv_spec, lv_spec],
        out_specs=[fwd, bwd],
        out_shape=[out, out],
        scratch_shapes=[state, state],
        compiler_params=pltpu.CompilerParams(
            dimension_semantics=("parallel", "parallel", "arbitrary"), vmem_limit_bytes=VMEM_LIMIT),
        name="hgrn",
    )(hq, hi, hff, hq, hi, hfb, lbf, lbb, lvf, lvb)


def _merge_kernel(x_ref, ge_ref, be_ref, yna_ref, of_ref, ob_ref, hog_ref, ng_ref, ymem_ref,
                  wg0_ref, wg1_ref, wna_ref, whg_ref, wmem_ref, wout_ref, g1_ref, b1_ref, o_ref):
    wg_refs = (wg0_ref, wg1_ref)
    gate_blk = BRANCH_COLS // W_IN_BLOCK
    for rows in _sub_tiles(x_ref.shape[0]):
        xn = _layer_norm(x_ref[rows, :], ge_ref[...], be_ref[...])
        xb = xn.astype(BF16)

        heads = []
        for h in range(HG_HEADS):
            o = of_ref[h, rows, :] + ob_ref[h, rows, :]
            heads.append(o * lax.rsqrt(jnp.mean(o * o, axis=-1, keepdims=True) + RMS_EPS))
        y_hg = jnp.concatenate(heads, axis=-1) * ng_ref[...] * jax.nn.silu(hog_ref[rows, :])

        def gate(j):
            half = D_MODEL // 2
            pre = [_dot(xb, _w_in_cols(wg_refs, gate_blk, BRANCH_COLS + j * D_MODEL + c, half))
                   for c in (0, half)]
            return jax.nn.sigmoid(jnp.concatenate(pre, axis=-1))

        merged = gate(0) * _dot(yna_ref[rows, :], wna_ref[...])
        merged = merged + gate(1) * _dot(y_hg.astype(BF16), whg_ref[...])
        merged = merged + gate(2) * _dot(ymem_ref[rows, :], wmem_ref[...])
        y = ALPHA * xn + _dot(merged.astype(BF16), wout_ref[...])
        o_ref[rows, :] = _layer_norm(y, g1_ref[...], b1_ref[...])


def _merge(x2, ge, be, y_na, o_f, o_b, hog, ng, y_mem, w_in_b, wna, whg, wmem, wout, g1, b1):
    T = x2.shape[0]
    tm = TOKEN_TILE
    row = lambda n: pl.BlockSpec((tm, n), lambda i: (i, 0))
    vec = lambda n: _resident((1, n))
    head = pl.BlockSpec((HG_HEADS, tm, HG_DIM), lambda i: (0, i, 0))
    gate_blk = BRANCH_COLS // W_IN_BLOCK
    assert w_in_b.shape[1] == (gate_blk + 2) * W_IN_BLOCK
    w_blk = lambda j: pl.BlockSpec((D_MODEL, W_IN_BLOCK), lambda i: (0, j), pipeline_mode=pl.Buffered(1))
    return pl.pallas_call(
        _merge_kernel,
        grid=(T // tm,),
        in_specs=[
            row(D_MODEL), vec(D_MODEL), vec(D_MODEL),
            row(NA_WIDTH), head, head, row(HG_WIDTH), vec(HG_WIDTH), row(MEM_WIDTH),
            w_blk(gate_blk), w_blk(gate_blk + 1), _resident((NA_WIDTH, D_MODEL)), _resident((HG_WIDTH, D_MODEL)),
            _resident((MEM_WIDTH, D_MODEL)), _resident((D_MODEL, D_MODEL)), vec(D_MODEL), vec(D_MODEL),
        ],
        out_specs=row(D_MODEL),
        out_shape=jax.ShapeDtypeStruct((T, D_MODEL), F32),
        compiler_params=pltpu.CompilerParams(
            dimension_semantics=("parallel",), vmem_limit_bytes=VMEM_LIMIT),
        name="merge",
    )(x2, ge, be, y_na, o_f, o_b, hog, ng, y_mem, w_in_b, w_in_b, wna, whg, wmem, wout, g1, b1)


def _ffn_kernel(x_ref, w1_ref, w2_ref, g_ref, b_ref, o_ref):
    for rows in _sub_tiles(x_ref.shape[0]):
        x = x_ref[rows, :]
        xb = x.astype(BF16)
        acc = ALPHA * x
        for j in range(D_FF // D_MODEL):
            sl = slice(j * D_MODEL, (j + 1) * D_MODEL)
            h = jnp.maximum(_dot(xb, w1_ref[:, sl]), 0.0)
            acc = acc + _dot((h * h).astype(BF16), w2_ref[sl, :])
        o_ref[rows, :] = _layer_norm(acc, g_ref[...], b_ref[...])


def _ffn(x1, w1, w2, g, b):
    T = x1.shape[0]
    tm = TOKEN_TILE
    return pl.pallas_call(
        _ffn_kernel,
        grid=(T // tm,),
        in_specs=[
            pl.BlockSpec((tm, D_MODEL), lambda i: (i, 0)),
            _resident((D_MODEL, D_FF)), _resident((D_FF, D_MODEL)),
            _resident((1, D_MODEL)), _resident((1, D_MODEL)),
        ],
        out_specs=pl.BlockSpec((tm, D_MODEL), lambda i: (i, 0)),
        out_shape=jax.ShapeDtypeStruct((T, D_MODEL), F32),
        compiler_params=pltpu.CompilerParams(
            dimension_semantics=("parallel",), vmem_limit_bytes=VMEM_LIMIT),
        name="ffn",
    )(x1, w1, w2, g, b)


def kernel(x, mem, ln_emb_g, ln_emb_b, w_in, na_rpb, hg_lb_logits, hg_norm_g, w_mem_kv, w_branch_na,
           w_branch_hg, w_branch_mem, w_out, ln1_g, ln1_b, w_ff1, w_ff2, ln2_g, ln2_b):
    B, S, D = x.shape
    assert D == D_MODEL and S % HG_CHUNK == 0 and S % GRID_W == 0 and S // GRID_W >= NA_KH
    assert w_in.shape[0] == DEPTH and (B * S) % TOKEN_TILE == 0
    T = B * S
    l = 0
    vec = lambda a: a.reshape(1, -1).astype(F32)

    w_in_b = w_in[l].astype(BF16)
    lb_all = jnp.cumsum(jax.nn.softmax(hg_lb_logits.astype(F32), axis=1), axis=1)
    lbf = lb_all[0, l].reshape(HG_HEADS, 1, HG_DIM)
    lbb = lb_all[1, l].reshape(HG_HEADS, 1, HG_DIM)
    bands = _na_bias_bands(na_rpb[l])

    x2 = x.reshape(T, D)
    ge, be = vec(ln_emb_g), vec(ln_emb_b)
    qkv, hq, hi, hog, hff, hfb, y_mem = _ln_proj(x2, ge, be, w_in_b, mem, w_mem_kv[l].astype(BF16), S)
    y_na = _na_attn(qkv.reshape(B, S, 3 * NA_WIDTH), bands).reshape(T, NA_WIDTH)
    o_f, o_b = _hgrn(hq, hi, hff, hfb, lbf, lbb, B)
    x1 = _merge(x2, ge, be, y_na, o_f, o_b, hog, vec(hg_norm_g[l]), y_mem, w_in_b,
                w_branch_na[l].astype(BF16), w_branch_hg[l].astype(BF16), w_branch_mem[l].astype(BF16),
                w_out[l].astype(BF16), vec(ln1_g[l]), vec(ln1_b[l]))
    out = _ffn(x1, w_ff1[l].astype(BF16), w_ff2[l].astype(BF16), vec(ln2_g[l]), vec(ln2_b[l]))
    return out.reshape(B, S, D)
```

```python
import functools

import numpy as np
import jax
import jax.numpy as jnp
from jax import lax
from jax.experimental import pallas as pl
from jax.experimental.pallas import tpu as pltpu

F32 = jnp.float32
BF16 = jnp.bfloat16

D_MODEL = 1024
GRID_W = 64
NA_HEADS = 8
NA_HEAD_DIM = 64
NA_WIDTH = NA_HEADS * NA_HEAD_DIM
NA_KH = 8
NA_KW = 16
HG_HEADS = 4
HG_DIM = 128
HG_WIDTH = HG_HEADS * HG_DIM
MEM_HEADS = 4
MEM_HEAD_DIM = 128
MEM_WIDTH = MEM_HEADS * MEM_HEAD_DIM
D_FF = 4 * D_MODEL
DEPTH = 1
ALPHA = (2.0 * DEPTH) ** 0.25
LN_EPS = 1e-5
RMS_EPS = 1e-6
BRANCH_COLS = 9 * 512
W_IN_BLOCK = 1536
MASK_NEG = -1e30

LANES = 128
SUBLANES = 8
VMEM_LIMIT = 56 * 1024 * 1024
TOKEN_TILE = 1024
SUB_TILE = 256
HG_CHUNK = 256
HG_HPS = 4
HG_LEVELS = 7
NA_GROUP = 4
NA_WIN = 12
NA_PAIRS = 22
NA_UNROLL = 8


def _layer_norm(x, g, b):
    mu = jnp.mean(x, axis=-1, keepdims=True)
    xc = x - mu
    var = jnp.mean(xc * xc, axis=-1, keepdims=True)
    return xc * lax.rsqrt(var + LN_EPS) * g + b


def _sub_tiles(tm):
    return [slice(r, r + SUB_TILE) for r in range(0, tm, SUB_TILE)]


def _resident(shape):
    return pl.BlockSpec(shape, lambda *_: (0,) * len(shape), pipeline_mode=pl.Buffered(1))


def _dot(a, b):
    return jnp.dot(a, b, preferred_element_type=F32)


def _dot_nt(a, b):
    return lax.dot_general(a, b, (((1,), (1,)), ((), ())), preferred_element_type=F32)


def _dot_tn(a, b):
    return lax.dot_general(a, b, (((0,), (0,)), ((), ())), preferred_element_type=F32)


def _w_in_cols(w_refs, first_block, c0, n):
    blk, off = divmod(c0 - first_block * W_IN_BLOCK, W_IN_BLOCK)
    assert off + n <= W_IN_BLOCK
    return w_refs[blk][:, off:off + n]


def _ln_proj_kernel(x_ref, g_ref, b_ref, w0_ref, w1_ref, w2_ref, mem_ref, wkv_ref,
                    qkv_ref, hq_ref, hi_ref, hog_ref, hff_ref, hfb_ref, ymem_ref, k_scr, v_scr, *, tiles_per_seq):
    @pl.when(pl.program_id(0) % tiles_per_seq == 0)
    def _():
        mb = mem_ref[0].astype(BF16)
        k_scr[...] = _dot(mb, wkv_ref[:, :MEM_WIDTH]).astype(BF16)
        v_scr[...] = _dot(mb, wkv_ref[:, MEM_WIDTH:]).astype(BF16)

    w_refs = (w0_ref, w1_ref, w2_ref)
    mem_scale = MEM_HEAD_DIM ** -0.5
    for rows in _sub_tiles(x_ref.shape[0]):
        xb = _layer_norm(x_ref[rows, :], g_ref[...], b_ref[...]).astype(BF16)

        def cols(j):
            return _dot(xb, _w_in_cols(w_refs, 0, j * 512, 512))

        for j in range(3):
            qkv_ref[rows, j * 512:(j + 1) * 512] = cols(j).astype(BF16)
        for j, ref in ((3, hq_ref), (4, hi_ref), (6, hff_ref), (7, hfb_ref)):
            r = cols(j)
            for h in range(HG_HEADS):
                ref[h, rows, :] = r[:, h * HG_DIM:(h + 1) * HG_DIM].astype(ref.dtype)
        hog_ref[rows, :] = cols(5)

        mq = cols(8).astype(BF16)
        for h in range(MEM_HEADS):
            sl = slice(h * MEM_HEAD_DIM, (h + 1) * MEM_HEAD_DIM)
            s = _dot_nt(mq[:, sl], k_scr[:, sl]) * mem_scale
            m = jnp.max(s, axis=-1, keepdims=True)
            p = jnp.exp(s - m)
            l = jnp.sum(p, axis=-1, keepdims=True)
            ymem_ref[rows, sl] = (_dot(p.astype(BF16), v_scr[:, sl]) / l).astype(ymem_ref.dtype)


def _ln_proj(x2, g, b, w_in_b, mem, wkv, seq):
    T = x2.shape[0]
    tm = TOKEN_TILE
    assert seq % tm == 0 and BRANCH_COLS == 3 * W_IN_BLOCK
    tiles_per_seq = seq // tm
    M = mem.shape[1]
    head_spec = pl.BlockSpec((HG_HEADS, tm, HG_DIM), lambda i: (0, i, 0))
    head_f32 = jax.ShapeDtypeStruct((HG_HEADS, T, HG_DIM), F32)
    w_blk = lambda j: pl.BlockSpec((D_MODEL, W_IN_BLOCK), lambda i: (0, j), pipeline_mode=pl.Buffered(1))
    return pl.pallas_call(
        functools.partial(_ln_proj_kernel, tiles_per_seq=tiles_per_seq),
        grid=(T // tm,),
        in_specs=[
            pl.BlockSpec((tm, D_MODEL), lambda i: (i, 0)),
            _resident((1, D_MODEL)), _resident((1, D_MODEL)), w_blk(0), w_blk(1), w_blk(2),
            pl.BlockSpec((1, M, D_MODEL), lambda i: (i // tiles_per_seq, 0, 0)),
            _resident((D_MODEL, 2 * MEM_WIDTH)),
        ],
        out_specs=[
            pl.BlockSpec((tm, 3 * NA_WIDTH), lambda i: (i, 0)),
            head_spec, head_spec,
            pl.BlockSpec((tm, HG_WIDTH), lambda i: (i, 0)),
            head_spec, head_spec,
            pl.BlockSpec((tm, MEM_WIDTH), lambda i: (i, 0)),
        ],
        out_shape=[
            jax.ShapeDtypeStruct((T, 3 * NA_WIDTH), BF16),
            head_f32,
            jax.ShapeDtypeStruct((HG_HEADS, T, HG_DIM), BF16),
            jax.ShapeDtypeStruct((T, HG_WIDTH), F32),
            head_f32, head_f32,
            jax.ShapeDtypeStruct((T, MEM_WIDTH), BF16),
        ],
        scratch_shapes=[pltpu.VMEM((M, MEM_WIDTH), BF16), pltpu.VMEM((M, MEM_WIDTH), BF16)],
        compiler_params=pltpu.CompilerParams(
            dimension_semantics=("arbitrary",), vmem_limit_bytes=VMEM_LIMIT),
        name="ln_proj",
    )(x2, g, b, w_in_b, w_in_b, w_in_b, mem, wkv)


def _na_bias_bands(rpb):
    qc = np.arange(GRID_W)[None, :]
    kc = np.arange(GRID_W)[:, None]
    c0 = np.clip(qc - NA_KW // 2, 0, GRID_W - NA_KW)
    valid = (kc >= c0) & (kc < c0 + NA_KW)
    dc = kc - qc + NA_KW - 1
    onehot = ((dc[None] == np.arange(2 * NA_KW - 1)[:, None, None]) & valid[None]).astype(np.float32)
    bnd = jnp.einsum("hdj,jkq->hdkq", rpb.astype(F32), jnp.asarray(onehot), precision=lax.Precision.HIGHEST)
    bnd = bnd + jnp.asarray(np.where(valid, 0.0, MASK_NEG).astype(np.float32))
    pad = jnp.full((NA_HEADS, 4, GRID_W, GRID_W), MASK_NEG, F32)
    return jnp.concatenate([pad, bnd, pad], axis=1)


def _na_kernel(q_ref, k_ref, v_ref, band_ref, o_ref, tab_ref, *, rows):
    @pl.when(pl.program_id(1) == 0)
    def _():
        masked = jnp.full((GRID_W, GRID_W), MASK_NEG, F32)
        cat = lambda a, b: jnp.concatenate([a, b], axis=-1)
        for h in range(2):
            for i in range(NA_PAIRS):
                left, right = band_ref[h, i + 1], band_ref[h, i]
                tab_ref[h, i] = cat(left, right)
                tab_ref[h, NA_PAIRS + i] = cat(left, masked)
                tab_ref[h, 2 * NA_PAIRS + i] = cat(masked, right)
            tab_ref[h, 3 * NA_PAIRS] = cat(masked, masked)

    scale = NA_HEAD_DIM ** -0.5
    assert np.log2(scale) == np.round(np.log2(scale))
    gq = NA_GROUP * GRID_W
    first = lax.broadcasted_iota(jnp.int32, (gq, LANES), 1) < NA_HEAD_DIM

    def window_start(g):
        return int(np.clip(g * NA_GROUP - NA_KH // 2, 0, rows - NA_WIN))

    def scores(g):
        r0, ws = g * NA_GROUP, window_start(g)
        q = q_ref[0, r0 * GRID_W:r0 * GRID_W + gq, :] * jnp.asarray(scale, BF16)
        kw = k_ref[0, ws * GRID_W:(ws + NA_WIN) * GRID_W, :]
        zero = jnp.zeros_like(q)
        qs = jnp.concatenate([jnp.where(first, q, zero), jnp.where(first, zero, q)], axis=0)
        return _dot_nt(kw, qs)

    def table_index(g, jp, a):
        r, kr = g * NA_GROUP + 2 * jp, window_start(g) + a
        in_window = lambda row: 0 <= kr - int(np.clip(row - NA_KH // 2, 0, rows - NA_KH)) < NA_KH
        lv, rv = in_window(r), in_window(r + 1)
        i = kr - r + (NA_KH - 1) + 3
        return i if lv and rv else NA_PAIRS + i if lv else 2 * NA_PAIRS + i if rv else 3 * NA_PAIRS

    def softmax(g, s_t):
        ps, inv_ls = [], []
        for h in range(2):
            for jp in range(NA_GROUP // 2):
                live = [a for a in range(NA_WIN) if table_index(g, jp, a) != 3 * NA_PAIRS]
                a0, a1 = live[0], live[-1] + 1
                assert live == list(range(a0, a1))
                bias = jnp.concatenate([tab_ref[h, table_index(g, jp, a)] for a in range(a0, a1)], axis=0)
                lo = (h * (NA_GROUP // 2) + jp) * LANES
                s = (s_t[a0 * GRID_W:a1 * GRID_W, lo:lo + LANES] + bias).reshape(a1 - a0, GRID_W, LANES)
                m = jnp.max(jnp.max(s, axis=0), axis=0, keepdims=True)
                p = jnp.exp(s - m)
                inv_ls.append(1.0 / jnp.sum(jnp.sum(p, axis=0), axis=0, keepdims=True))
                blocks = [jnp.zeros((a0 * GRID_W, LANES), BF16),
                          p.astype(BF16).reshape((a1 - a0) * GRID_W, LANES),
                          jnp.zeros(((NA_WIN - a1) * GRID_W, LANES), BF16)]
                ps.append(jnp.concatenate([b for b in blocks if b.shape[0]], axis=0))
        return jnp.concatenate(ps, axis=1), jnp.concatenate(inv_ls, axis=1)

    def output(g, p_t, inv_l):
        r0, ws = g * NA_GROUP, window_start(g)
        vw = v_ref[0, ws * GRID_W:(ws + NA_WIN) * GRID_W, :]
        o_t = _dot_tn(vw, p_t) * inv_l
        o = jnp.concatenate([o_t[:NA_HEAD_DIM, :gq], o_t[NA_HEAD_DIM:, gq:]], axis=0).T
        o_ref[0, r0 * GRID_W:r0 * GRID_W + gq, :] = o.astype(o_ref.dtype)

    n_groups = rows // NA_GROUP
    s_next = scores(0)
    for g in range(n_groups):
        s_cur = s_next
        if g + 1 < n_groups:
            s_next = scores(g + 1)
        output(g, *softmax(g, s_cur))


def _na_attn(qkv3, bands):
    B, S, _ = qkv3.shape
    rows = S // GRID_W
    assert rows % NA_GROUP == 0 and rows >= NA_WIN and bands.shape[1] == NA_PAIRS + 1
    n_pairs = NA_HEADS // 2
    blk = lambda c0: pl.BlockSpec((1, S, LANES), lambda hp, b: (b, 0, c0 + hp))
    return pl.pallas_call(
        functools.partial(_na_kernel, rows=rows),
        grid=(n_pairs, B),
        in_specs=[
            blk(0), blk(n_pairs), blk(2 * n_pairs),
            pl.BlockSpec((2, NA_PAIRS + 1, GRID_W, GRID_W), lambda hp, b: (hp, 0, 0, 0)),
        ],
        out_specs=pl.BlockSpec((1, S, LANES), lambda hp, b: (b, 0, hp)),
        out_shape=jax.ShapeDtypeStruct((B, S, NA_WIDTH), BF16),
        scratch_shapes=[pltpu.VMEM((2, 3 * NA_PAIRS + 1, GRID_W, LANES), F32)],
        compiler_params=pltpu.CompilerParams(
            dimension_semantics=("arbitrary", "arbitrary"), vmem_limit_bytes=VMEM_LIMIT),
        name="na_attn",
    )(qkv3, qkv3, qkv3, bands)


def _hgrn_scales_step(row, col, n, reverse):
    C = row.shape[0]
    nv = C // SUBLANES
    if n < SUBLANES:
        r3 = row.reshape(nv, SUBLANES, LANES)
        c3 = col.reshape(nv, SUBLANES, LANES)
        sub = lax.broadcasted_iota(jnp.int32, (nv, SUBLANES, LANES), 1)
        in_g = (sub & n) != 0
        bc = lambda i: jnp.broadcast_to(r3[:, i:i + 1, :], r3.shape)
        if n == 1:
            up = pltpu.roll(r3, 1, axis=1)
            dn = pltpu.roll(r3, SUBLANES - 1, axis=1)
            if not reverse:
                new_r = jnp.where(in_g, r3 * up, r3)
                new_c = jnp.where(in_g, c3, c3 * dn)
            else:
                new_r = jnp.where(in_g, r3, r3 * dn)
                new_c = jnp.where(in_g, c3 * up, c3)
        else:
            lo = sub < 4
            if not reverse:
                if n == 2:
                    tf = jnp.where(lo, bc(1), bc(5))
                    tg = jnp.where(lo, bc(3), bc(7))
                else:
                    tf, tg = bc(3), bc(7)
            else:
                if n == 2:
                    tf = jnp.where(lo, bc(0), bc(4))
                    tg = jnp.where(lo, bc(2), bc(6))
                else:
                    tf, tg = bc(0), bc(4)
            if not reverse:
                new_r = jnp.where(in_g, r3 * tf, r3)
                new_c = jnp.where(in_g, c3, c3 * tg)
            else:
                new_r = jnp.where(in_g, r3, r3 * tg)
                new_c = jnp.where(in_g, c3 * tf, c3)
        return new_r.reshape(C, LANES), new_c.reshape(C, LANES)

    m = n // SUBLANES
    nb = nv // (2 * m)
    r5 = row.reshape(nb, 2, m, SUBLANES, LANES)
    c5 = col.reshape(nb, 2, m, SUBLANES, LANES)
    rf, rg = r5[:, 0], r5[:, 1]
    cf, cg = c5[:, 0], c5[:, 1]
    if not reverse:
        tf = rf[:, m - 1:m, SUBLANES - 1:SUBLANES, :]
        tg = rg[:, m - 1:m, SUBLANES - 1:SUBLANES, :]
        new_r = jnp.stack([rf, rg * tf], axis=1)
        new_c = jnp.stack([cf * tg, cg], axis=1)
    else:
        tf = rf[:, 0:1, 0:1, :]
        tg = rg[:, 0:1, 0:1, :]
        new_r = jnp.stack([rf * tg, rg], axis=1)
        new_c = jnp.stack([cf, cg * tf], axis=1)
    return new_r.reshape(C, LANES), new_c.reshape(C, LANES)


def _hgrn_direction(q, fpre, lb, v_bf, state_ref, level, reverse):
    C = q.shape[0]
    hc = C // 2
    halves = (slice(0, hc), slice(hc, C))
    sig = jax.nn.sigmoid(fpre)
    f = lb + (1.0 - lb) * sig
    k = (1.0 - lb) * jax.nn.sigmoid(-fpre)

    qb, kb = q.astype(BF16), k.astype(BF16)
    a_diag = [jnp.where(level == HG_LEVELS, _dot_nt(qb[h], kb[h]), 0.0) for h in halves]
    row, col = f, jnp.ones_like(f)
    n = 1
    for lv in range(HG_LEVELS):
        qs, ks = (q * row).astype(BF16), (k * col).astype(BF16)
        a_diag = [jnp.where(level == lv, _dot_nt(qs[h], ks[h]), a) for h, a in zip(halves, a_diag)]
        row, col = _hgrn_scales_step(row, col, n, reverse)
        n *= 2

    qs, ks = (q * row).astype(BF16), (k * col).astype(BF16)
    lo, hi = halves
    if not reverse:
        a_x = _dot_nt(qs[hi], ks[lo])
        o_lo = _dot(a_diag[0].astype(BF16), v_bf[lo])
        o_hi = _dot(jnp.concatenate([a_x, a_diag[1]], axis=1).astype(BF16), v_bf)
    else:
        a_x = _dot_nt(qs[lo], ks[hi])
        o_lo = _dot(jnp.concatenate([a_diag[0], a_x], axis=1).astype(BF16), v_bf)
        o_hi = _dot(a_diag[1].astype(BF16), v_bf[hi])
    row, col = _hgrn_scales_step(row, col, n, reverse)

    st = state_ref[...]
    o = jnp.concatenate([o_lo, o_hi], axis=0) + _dot_nt((q * row).astype(BF16), st.astype(BF16))
    total = row[C - 1:C, :] if not reverse else row[0:1, :]
    state_ref[...] = st * total + _dot_tn(v_bf, (k * col).astype(BF16))
    return o


def _hgrn_kernel(qf_ref, vf_ref, ff_ref, qb_ref, vb_ref, fb_ref, lbf_ref, lbb_ref, lvf_ref, lvb_ref,
                 of_ref, ob_ref, sf_ref, sb_ref):
    @pl.when(pl.program_id(2) == 0)
    def _():
        sf_ref[...] = jnp.zeros_like(sf_ref)
        sb_ref[...] = jnp.zeros_like(sb_ref)

    for h in range(HG_HPS):
        qf = jax.nn.silu(qf_ref[h])
        of_ref[h] = _hgrn_direction(qf, ff_ref[h], lbf_ref[h], vf_ref[h], sf_ref.at[h], lvf_ref[...], False)
        qb = jax.nn.silu(qb_ref[h])
        ob_ref[h] = _hgrn_direction(qb, fb_ref[h], lbb_ref[h], vb_ref[h], sb_ref.at[h], lvb_ref[...], True)


def _hgrn_level_tables():
    hc = HG_CHUNK // 2
    t = np.arange(hc)[:, None]
    s = np.arange(hc)[None, :]
    x = t ^ s
    hb = np.where(x > 0, np.floor(np.log2(np.maximum(x, 1))), HG_LEVELS).astype(np.int32)
    return jnp.asarray(np.where(t >= s, hb, -1)), jnp.asarray(np.where(t <= s, hb, -1))


def _hgrn(hq, hi, hff, hfb, lbf, lbb, batch):
    H, T, _ = hq.shape
    C = HG_CHUNK
    nc = T // batch // C
    fwd = pl.BlockSpec((HG_HPS, C, HG_DIM), lambda b, h, c: (h, b * nc + c, 0))
    bwd = pl.BlockSpec((HG_HPS, C, HG_DIM), lambda b, h, c: (h, b * nc + nc - 1 - c, 0))
    lb_spec = pl.BlockSpec((HG_HPS, 1, HG_DIM), lambda b, h, c: (h, 0, 0))
    lv_spec = _resident((C // 2, C // 2))
    lvf, lvb = _hgrn_level_tables()
    out = jax.ShapeDtypeStruct((H, T, HG_DIM), F32)
    state = pltpu.VMEM((HG_HPS, HG_DIM, HG_DIM), F32)
    return pl.pallas_call(
        _hgrn_kernel,
        grid=(batch, H // HG_HPS, nc),
        in_specs=[fwd, fwd, fwd, bwd, bwd, bwd, lb_spec, lb_spec, lv_spec, lv_spec],
        out_specs=[fwd, bwd],
        out_shape=[out, out],
        scratch_shapes=[state, state],
        compiler_params=pltpu.CompilerParams(
            dimension_semantics=("parallel", "parallel", "arbitrary"), vmem_limit_bytes=VMEM_LIMIT),
        name="hgrn",
    )(hq, hi, hff, hq, hi, hfb, lbf, lbb, lvf, lvb)


def _merge_kernel(x_ref, ge_ref, be_ref, yna_ref, of_ref, ob_ref, hog_ref, ng_ref, ymem_ref,
                  wg0_ref, wg1_ref, wna_ref, whg_ref, wmem_ref, wout_ref, g1_ref, b1_ref, o_ref):
    wg_refs = (wg0_ref, wg1_ref)
    for rows in _sub_tiles(x_ref.shape[0]):
        xn = _layer_norm(x_ref[rows, :], ge_ref[...], be_ref[...])
        xb = xn.astype(BF16)

        heads = []
        for h in range(HG_HEADS):
            o = of_ref[h, rows, :] + ob_ref[h, rows, :]
            heads.append(o * lax.rsqrt(jnp.mean(o * o, axis=-1, keepdims=True) + RMS_EPS))
        y_hg = jnp.concatenate(heads, axis=-1) * ng_ref[...] * jax.nn.silu(hog_ref[rows, :])

        pre = jnp.concatenate([_dot(xb, w[...]) for w in wg_refs], axis=-1)

        def gate(j):
            return jax.nn.sigmoid(pre[:, j * D_MODEL:(j + 1) * D_MODEL])

        merged = gate(0) * _dot(yna_ref[rows, :], wna_ref[...])
        merged = merged + gate(1) * _dot(y_hg.astype(BF16), whg_ref[...])
        merged = merged + gate(2) * _dot(ymem_ref[rows, :], wmem_ref[...])
        y = ALPHA * xn + _dot(merged.astype(BF16), wout_ref[...])
        o_ref[rows, :] = _layer_norm(y, g1_ref[...], b1_ref[...])


def _merge(x2, ge, be, y_na, o_f, o_b, hog, ng, y_mem, w_in_b, wna, whg, wmem, wout, g1, b1):
    T = x2.shape[0]
    tm = TOKEN_TILE
    row = lambda n: pl.BlockSpec((tm, n), lambda i: (i, 0))
    vec = lambda n: _resident((1, n))
    head = pl.BlockSpec((HG_HEADS, tm, HG_DIM), lambda i: (0, i, 0))
    gate_blk = BRANCH_COLS // W_IN_BLOCK
    assert w_in_b.shape[1] == (gate_blk + 2) * W_IN_BLOCK
    w_blk = lambda j: pl.BlockSpec((D_MODEL, W_IN_BLOCK), lambda i: (0, j), pipeline_mode=pl.Buffered(1))
    return pl.pallas_call(
        _merge_kernel,
        grid=(T // tm,),
        in_specs=[
            row(D_MODEL), vec(D_MODEL), vec(D_MODEL),
            row(NA_WIDTH), head, head, row(HG_WIDTH), vec(HG_WIDTH), row(MEM_WIDTH),
            w_blk(gate_blk), w_blk(gate_blk + 1), _resident((NA_WIDTH, D_MODEL)), _resident((HG_WIDTH, D_MODEL)),
            _resident((MEM_WIDTH, D_MODEL)), _resident((D_MODEL, D_MODEL)), vec(D_MODEL), vec(D_MODEL),
        ],
        out_specs=row(D_MODEL),
        out_shape=jax.ShapeDtypeStruct((T, D_MODEL), F32),
        compiler_params=pltpu.CompilerParams(
            dimension_semantics=("parallel",), vmem_limit_bytes=VMEM_LIMIT),
        name="merge",
    )(x2, ge, be, y_na, o_f, o_b, hog, ng, y_mem, w_in_b, w_in_b, wna, whg, wmem, wout, g1, b1)


def _ffn_kernel(x_ref, w1_ref, w2_ref, g_ref, b_ref, o_ref):
    for rows in _sub_tiles(x_ref.shape[0]):
        x = x_ref[rows, :]
        xb = x.astype(BF16)
        acc = ALPHA * x
        for j in range(D_FF // D_MODEL):
            sl = slice(j * D_MODEL, (j + 1) * D_MODEL)
            h = jnp.maximum(_dot(xb, w1_ref[:, sl]), 0.0)
            acc = acc + _dot((h * h).astype(BF16), w2_ref[sl, :])
        o_ref[rows, :] = _layer_norm(acc, g_ref[...], b_ref[...])


def _ffn(x1, w1, w2, g, b):
    T = x1.shape[0]
    tm = TOKEN_TILE
    return pl.pallas_call(
        _ffn_kernel,
        grid=(T // tm,),
        in_specs=[
            pl.BlockSpec((tm, D_MODEL), lambda i: (i, 0)),
            _resident((D_MODEL, D_FF)), _resident((D_FF, D_MODEL)),
            _resident((1, D_MODEL)), _resident((1, D_MODEL)),
        ],
        out_specs=pl.BlockSpec((tm, D_MODEL), lambda i: (i, 0)),
        out_shape=jax.ShapeDtypeStruct((T, D_MODEL), F32),
        compiler_params=pltpu.CompilerParams(
            dimension_semantics=("parallel",), vmem_limit_bytes=VMEM_LIMIT),
        name="ffn",
    )(x1, w1, w2, g, b)


def kernel(x, mem, ln_emb_g, ln_emb_b, w_in, na_rpb, hg_lb_logits, hg_norm_g, w_mem_kv, w_branch_na,
           w_branch_hg, w_branch_mem, w_out, ln1_g, ln1_b, w_ff1, w_ff2, ln2_g, ln2_b):
    B, S, D = x.shape
    assert D == D_MODEL and S % HG_CHUNK == 0 and S % GRID_W == 0 and S // GRID_W >= NA_KH
    assert w_in.shape[0] == DEPTH and (B * S) % TOKEN_TILE == 0
    T = B * S
    l = 0
    vec = lambda a: a.reshape(1, -1).astype(F32)

    w_in_b = w_in[l].astype(BF16)
    lb_all = jnp.cumsum(jax.nn.softmax(hg_lb_logits.astype(F32), axis=1), axis=1)
    lbf = lb_all[0, l].reshape(HG_HEADS, 1, HG_DIM)
    lbb = lb_all[1, l].reshape(HG_HEADS, 1, HG_DIM)
    bands = _na_bias_bands(na_rpb[l])

    x2 = x.reshape(T, D)
    ge, be = vec(ln_emb_g), vec(ln_emb_b)
    qkv, hq, hi, hog, hff, hfb, y_mem = _ln_proj(x2, ge, be, w_in_b, mem, w_mem_kv[l].astype(BF16), S)
    y_na = _na_attn(qkv.reshape(B, S, 3 * NA_WIDTH), bands).reshape(T, NA_WIDTH)
    o_f, o_b = _hgrn(hq, hi, hff, hfb, lbf, lbb, B)
    x1 = _merge(x2, ge, be, y_na, o_f, o_b, hog, vec(hg_norm_g[l]), y_mem, w_in_b,
                w_branch_na[l].astype(BF16), w_branch_hg[l].astype(BF16), w_branch_mem[l].astype(BF16),
                w_out[l].astype(BF16), vec(ln1_g[l]), vec(ln1_b[l]))
    out = _ffn(x1, w_ff1[l].astype(BF16), w_ff2[l].astype(BF16), vec(ln2_g[l]), vec(ln2_b[l]))
    return out.reshape(B, S, D)
```

```python
import functools

import numpy as np
import jax
import jax.numpy as jnp
from jax import lax
from jax.experimental import pallas as pl
from jax.experimental.pallas import tpu as pltpu

F32 = jnp.float32
BF16 = jnp.bfloat16

D_MODEL = 1024
GRID_W = 64
NA_HEADS = 8
NA_HEAD_DIM = 64
NA_WIDTH = NA_HEADS * NA_HEAD_DIM
NA_KH = 8
NA_KW = 16
HG_HEADS = 4
HG_DIM = 128
HG_WIDTH = HG_HEADS * HG_DIM
MEM_HEADS = 4
MEM_HEAD_DIM = 128
MEM_WIDTH = MEM_HEADS * MEM_HEAD_DIM
D_FF = 4 * D_MODEL
DEPTH = 1
ALPHA = (2.0 * DEPTH) ** 0.25
LN_EPS = 1e-5
RMS_EPS = 1e-6
BRANCH_COLS = 9 * 512
W_IN_BLOCK = 1536
MASK_NEG = -1e30

LANES = 128
SUBLANES = 8
VMEM_LIMIT = 56 * 1024 * 1024
TOKEN_TILE = 1024
SUB_TILE = 256
HG_CHUNK = 256
HG_HPS = 4
HG_LEVELS = 7
NA_GROUP = 4
NA_WIN = 12
NA_PAIRS = 22
NA_UNROLL = 8


def _layer_norm(x, g, b):
    mu = jnp.mean(x, axis=-1, keepdims=True)
    xc = x - mu
    var = jnp.mean(xc * xc, axis=-1, keepdims=True)
    return xc * lax.rsqrt(var + LN_EPS) * g + b


def _sub_tiles(tm):
    return [slice(r, r + SUB_TILE) for r in range(0, tm, SUB_TILE)]


def _interleave(chains, skew=1):
    chains = list(chains)
    start = {id(c): skew * i for i, c in enumerate(chains)}
    rnd = 0
    while chains:
        for chain in list(chains):
            if rnd >= start[id(chain)] and next(chain, StopIteration) is StopIteration:
                chains.remove(chain)
        rnd += 1


def _resident(shape):
    return pl.BlockSpec(shape, lambda *_: (0,) * len(shape), pipeline_mode=pl.Buffered(1))


def _dot(a, b):
    return jnp.dot(a, b, preferred_element_type=F32)


def _dot_nt(a, b):
    return lax.dot_general(a, b, (((1,), (1,)), ((), ())), preferred_element_type=F32)


def _dot_tn(a, b):
    return lax.dot_general(a, b, (((0,), (0,)), ((), ())), preferred_element_type=F32)


def _w_in_cols(w_refs, first_block, c0, n):
    blk, off = divmod(c0 - first_block * W_IN_BLOCK, W_IN_BLOCK)
    assert off + n <= W_IN_BLOCK
    return w_refs[blk][:, off:off + n]


def _ln_proj_kernel(x_ref, g_ref, b_ref, w0_ref, w1_ref, w2_ref, mem_ref, wkv_ref,
                    qkv_ref, hq_ref, hi_ref, hog_ref, hff_ref, hfb_ref, ymem_ref, k_scr, v_scr, *, tiles_per_seq):
    @pl.when(pl.program_id(0) % tiles_per_seq == 0)
    def _():
        mb = mem_ref[0].astype(BF16)
        k_scr[...] = _dot(mb, wkv_ref[:, :MEM_WIDTH]).astype(BF16)
        v_scr[...] = _dot(mb, wkv_ref[:, MEM_WIDTH:]).astype(BF16)

    w_refs = (w0_ref, w1_ref, w2_ref)
    mem_scale = MEM_HEAD_DIM ** -0.5
    head_sl = [slice(h * MEM_HEAD_DIM, (h + 1) * MEM_HEAD_DIM) for h in range(MEM_HEADS)]

    def sub_tile(rows):
        xb = _layer_norm(x_ref[rows, :], g_ref[...], b_ref[...]).astype(BF16)
        yield

        def cols(j):
            return _dot(xb, _w_in_cols(w_refs, 0, j * 512, 512))

        def heads_out(j, ref):
            r = cols(j)
            for h in range(HG_HEADS):
                ref[h, rows, :] = r[:, h * HG_DIM:(h + 1) * HG_DIM].astype(ref.dtype)

        mq = cols(8).astype(BF16)
        scores = [_dot_nt(mq[:, sl], k_scr[:, sl]) * mem_scale for sl in head_sl]
        for j in range(3):
            qkv_ref[rows, j * 512:(j + 1) * 512] = cols(j).astype(BF16)
        yield

        probs = []
        for j, ref, heads in ((3, hq_ref, (0, 1)), (4, hi_ref, (2, 3))):
            for h in heads:
                p = jnp.exp(scores[h] - jnp.max(scores[h], axis=-1, keepdims=True))
                probs.append((p.astype(BF16), jnp.sum(p, axis=-1, keepdims=True)))
            yield
            heads_out(j, ref)
            yield

        hog_ref[rows, :] = cols(5)
        heads_out(6, hff_ref)
        heads_out(7, hfb_ref)
        for sl, (p, l) in zip(head_sl, probs):
            ymem_ref[rows, sl] = (_dot(p, v_scr[:, sl]) / l).astype(ymem_ref.dtype)

    _interleave([sub_tile(rows) for rows in _sub_tiles(x_ref.shape[0])])


def _ln_proj(x2, g, b, w_in_b, mem, wkv, seq):
    T = x2.shape[0]
    tm = TOKEN_TILE
    assert seq % tm == 0 and BRANCH_COLS == 3 * W_IN_BLOCK
    tiles_per_seq = seq // tm
    M = mem.shape[1]
    head_spec = pl.BlockSpec((HG_HEADS, tm, HG_DIM), lambda i: (0, i, 0))
    head_f32 = jax.ShapeDtypeStruct((HG_HEADS, T, HG_DIM), F32)
    w_blk = lambda j: pl.BlockSpec((D_MODEL, W_IN_BLOCK), lambda i: (0, j), pipeline_mode=pl.Buffered(1))
    return pl.pallas_call(
        functools.partial(_ln_proj_kernel, tiles_per_seq=tiles_per_seq),
        grid=(T // tm,),
        in_specs=[
            pl.BlockSpec((tm, D_MODEL), lambda i: (i, 0)),
            _resident((1, D_MODEL)), _resident((1, D_MODEL)), w_blk(0), w_blk(1), w_blk(2),
            pl.BlockSpec((1, M, D_MODEL), lambda i: (i // tiles_per_seq, 0, 0)),
            _resident((D_MODEL, 2 * MEM_WIDTH)),
        ],
        out_specs=[
            pl.BlockSpec((tm, 3 * NA_WIDTH), lambda i: (i, 0)),
            head_spec, head_spec,
            pl.BlockSpec((tm, HG_WIDTH), lambda i: (i, 0)),
            head_spec, head_spec,
            pl.BlockSpec((tm, MEM_WIDTH), lambda i: (i, 0)),
        ],
        out_shape=[
            jax.ShapeDtypeStruct((T, 3 * NA_WIDTH), BF16),
            head_f32,
            jax.ShapeDtypeStruct((HG_HEADS, T, HG_DIM), BF16),
            jax.ShapeDtypeStruct((T, HG_WIDTH), F32),
            head_f32, head_f32,
            jax.ShapeDtypeStruct((T, MEM_WIDTH), BF16),
        ],
        scratch_shapes=[pltpu.VMEM((M, MEM_WIDTH), BF16), pltpu.VMEM((M, MEM_WIDTH), BF16)],
        compiler_params=pltpu.CompilerParams(
            dimension_semantics=("arbitrary",), vmem_limit_bytes=VMEM_LIMIT),
        name="ln_proj",
    )(x2, g, b, w_in_b, w_in_b, w_in_b, mem, wkv)


def _na_bias_bands(rpb):
    qc = np.arange(GRID_W)[None, :]
    kc = np.arange(GRID_W)[:, None]
    c0 = np.clip(qc - NA_KW // 2, 0, GRID_W - NA_KW)
    valid = (kc >= c0) & (kc < c0 + NA_KW)
    dc = kc - qc + NA_KW - 1
    onehot = ((dc[None] == np.arange(2 * NA_KW - 1)[:, None, None]) & valid[None]).astype(np.float32)
    bnd = jnp.einsum("hdj,jkq->hdkq", rpb.astype(F32), jnp.asarray(onehot), precision=lax.Precision.HIGHEST)
    bnd = bnd + jnp.asarray(np.where(valid, 0.0, MASK_NEG).astype(np.float32))
    pad = jnp.full((NA_HEADS, 4, GRID_W, GRID_W), MASK_NEG, F32)
    return jnp.concatenate([pad, bnd, pad], axis=1)


def _na_kernel(q_ref, k_ref, v_ref, band_ref, *rest, rows, n_cast):
    cast_in, o_ref, cast_out, tab_ref = rest[:n_cast], rest[n_cast], rest[n_cast + 1:-1], rest[-1]
    for src, dst in zip(cast_in, cast_out):
        dst[...] = src[...].astype(dst.dtype)

    @pl.when(pl.program_id(1) == 0)
    def _():
        masked = jnp.full((GRID_W, GRID_W), MASK_NEG, F32)
        cat = lambda a, b: jnp.concatenate([a, b], axis=-1)
        for h in range(2):
            for i in range(NA_PAIRS):
                left, right = band_ref[h, i + 1], band_ref[h, i]
                tab_ref[h, i] = cat(left, right)
                tab_ref[h, NA_PAIRS + i] = cat(left, masked)
                tab_ref[h, 2 * NA_PAIRS + i] = cat(masked, right)
            tab_ref[h, 3 * NA_PAIRS] = cat(masked, masked)

    scale = NA_HEAD_DIM ** -0.5
    assert np.log2(scale) == np.round(np.log2(scale))
    gq = NA_GROUP * GRID_W
    first = lax.broadcasted_iota(jnp.int32, (gq, LANES), 1) < NA_HEAD_DIM

    def window_start(g):
        return int(np.clip(g * NA_GROUP - NA_KH // 2, 0, rows - NA_WIN))

    def scores(g):
        r0, ws = g * NA_GROUP, window_start(g)
        q = q_ref[0, r0 * GRID_W:r0 * GRID_W + gq, :] * jnp.asarray(scale, BF16)
        kw = k_ref[0, ws * GRID_W:(ws + NA_WIN) * GRID_W, :]
        zero = jnp.zeros_like(q)
        qs = jnp.concatenate([jnp.where(first, q, zero), jnp.where(first, zero, q)], axis=0)
        return _dot_nt(kw, qs)

    def table_index(g, jp, a):
        r, kr = g * NA_GROUP + 2 * jp, window_start(g) + a
        in_window = lambda row: 0 <= kr - int(np.clip(row - NA_KH // 2, 0, rows - NA_KH)) < NA_KH
        lv, rv = in_window(r), in_window(r + 1)
        i = kr - r + (NA_KH - 1) + 3
        return i if lv and rv else NA_PAIRS + i if lv else 2 * NA_PAIRS + i if rv else 3 * NA_PAIRS

    def softmax(g, s_t):
        ps, inv_ls = [], []
        for h in range(2):
            for jp in range(NA_GROUP // 2):
                live = [a for a in range(NA_WIN) if table_index(g, jp, a) != 3 * NA_PAIRS]
                a0, a1 = live[0], live[-1] + 1
                assert live == list(range(a0, a1))
                bias = jnp.concatenate([tab_ref[h, table_index(g, jp, a)] for a in range(a0, a1)], axis=0)
                lo = (h * (NA_GROUP // 2) + jp) * LANES
                s = (s_t[a0 * GRID_W:a1 * GRID_W, lo:lo + LANES] + bias).reshape(a1 - a0, GRID_W, LANES)
                m = jnp.max(jnp.max(s, axis=0), axis=0, keepdims=True)
                p = jnp.exp(s - m)
                inv_ls.append(1.0 / jnp.sum(jnp.sum(p, axis=0), axis=0, keepdims=True))
                blocks = [jnp.zeros((a0 * GRID_W, LANES), BF16),
                          p.astype(BF16).reshape((a1 - a0) * GRID_W, LANES),
                          jnp.zeros(((NA_WIN - a1) * GRID_W, LANES), BF16)]
                ps.append(jnp.concatenate([b for b in blocks if b.shape[0]], axis=0))
        return jnp.concatenate(ps, axis=1), jnp.concatenate(inv_ls, axis=1)

    def output(g, p_t, inv_l):
        r0, ws = g * NA_GROUP, window_start(g)
        vw = v_ref[0, ws * GRID_W:(ws + NA_WIN) * GRID_W, :]
        o_t = _dot_tn(vw, p_t) * inv_l
        o = jnp.concatenate([o_t[:NA_HEAD_DIM, :gq], o_t[NA_HEAD_DIM:, gq:]], axis=0).T
        o_ref[0, r0 * GRID_W:r0 * GRID_W + gq, :] = o.astype(o_ref.dtype)

    n_groups = rows // NA_GROUP
    s_next = scores(0)
    for g in range(n_groups):
        s_cur = s_next
        if g + 1 < n_groups:
            s_next = scores(g + 1)
        output(g, *softmax(g, s_cur))


def _na_attn(qkv3, bands, casts):
    B, S, _ = qkv3.shape
    rows = S // GRID_W
    assert rows % NA_GROUP == 0 and rows >= NA_WIN and bands.shape[1] == NA_PAIRS + 1
    n_pairs = NA_HEADS // 2
    steps = n_pairs * B
    blk = lambda c0: pl.BlockSpec((1, S, LANES), lambda hp, b: (b, 0, c0 + hp))
    cast_in, cast_out, cast_shape = [], [], []
    for w, c0, nc in casts:
        slab = w.shape[0] // steps
        assert slab * steps == w.shape[0] and slab % 16 == 0 and c0 % nc == 0
        cast_in.append(pl.BlockSpec((slab, nc), lambda hp, b, j=c0 // nc: (hp * B + b, j)))
        cast_out.append(pl.BlockSpec((slab, nc), lambda hp, b: (hp * B + b, 0)))
        cast_shape.append(jax.ShapeDtypeStruct((w.shape[0], nc), BF16))
    out = pl.pallas_call(
        functools.partial(_na_kernel, rows=rows, n_cast=len(casts)),
        grid=(n_pairs, B),
        in_specs=[
            blk(0), blk(n_pairs), blk(2 * n_pairs),
            pl.BlockSpec((2, NA_PAIRS + 1, GRID_W, GRID_W), lambda hp, b: (hp, 0, 0, 0)),
        ] + cast_in,
        out_specs=[pl.BlockSpec((1, S, LANES), lambda hp, b: (b, 0, hp))] + cast_out,
        out_shape=[jax.ShapeDtypeStruct((B, S, NA_WIDTH), BF16)] + cast_shape,
        scratch_shapes=[pltpu.VMEM((2, 3 * NA_PAIRS + 1, GRID_W, LANES), F32)],
        compiler_params=pltpu.CompilerParams(
            dimension_semantics=("arbitrary", "arbitrary"), vmem_limit_bytes=VMEM_LIMIT),
        name="na_attn",
    )(qkv3, qkv3, qkv3, bands, *[w for w, _, _ in casts])
    return out[0], out[1:]


def _hgrn_scales_step(row, col, n, reverse):
    C = row.shape[0]
    nv = C // SUBLANES
    if n < SUBLANES:
        r3 = row.reshape(nv, SUBLANES, LANES)
        c3 = col.reshape(nv, SUBLANES, LANES)
        sub = lax.broadcasted_iota(jnp.int32, (nv, SUBLANES, LANES), 1)
        in_g = (sub & n) != 0
        bc = lambda i: jnp.broadcast_to(r3[:, i:i + 1, :], r3.shape)
        if n == 1:
            up = pltpu.roll(r3, 1, axis=1)
            dn = pltpu.roll(r3, SUBLANES - 1, axis=1)
            if not reverse:
                new_r = jnp.where(in_g, r3 * up, r3)
                new_c = jnp.where(in_g, c3, c3 * dn)
            else:
                new_r = jnp.where(in_g, r3, r3 * dn)
                new_c = jnp.where(in_g, c3 * up, c3)
        else:
            lo = sub < 4
            if not reverse:
                if n == 2:
                    tf = jnp.where(lo, bc(1), bc(5))
                    tg = jnp.where(lo, bc(3), bc(7))
                else:
                    tf, tg = bc(3), bc(7)
            else:
                if n == 2:
                    tf = jnp.where(lo, bc(0), bc(4))
                    tg = jnp.where(lo, bc(2), bc(6))
                else:
                    tf, tg = bc(0), bc(4)
            if not reverse:
                new_r = jnp.where(in_g, r3 * tf, r3)
                new_c = jnp.where(in_g, c3, c3 * tg)
            else:
                new_r = jnp.where(in_g, r3, r3 * tg)
                new_c = jnp.where(in_g, c3 * tf, c3)
        return new_r.reshape(C, LANES), new_c.reshape(C, LANES)

    new_r, new_c = [], []
    for base in range(0, C, 2 * n):
        f_sl, g_sl = slice(base, base + n), slice(base + n, base + 2 * n)
        if not reverse:
            tf, tg = row[base + n - 1:base + n], row[base + 2 * n - 1:base + 2 * n]
            new_r += [row[f_sl], row[g_sl] * tf]
            new_c += [col[f_sl] * tg, col[g_sl]]
        else:
            tf, tg = row[base:base + 1], row[base + n:base + n + 1]
            new_r += [row[f_sl] * tg, row[g_sl]]
            new_c += [col[f_sl], col[g_sl] * tf]
    return jnp.concatenate(new_r, axis=0), jnp.concatenate(new_c, axis=0)


def _hgrn_direction(q, fpre, lb, v_bf, state_ref, level, reverse):
    C = q.shape[0]
    hc = C // 2
    halves = (slice(0, hc), slice(hc, C))
    f = lb + (1.0 - lb) * jax.nn.sigmoid(fpre)
    k = 1.0 - f

    def pick(lv, lhs, rhs, acc):
        return [jnp.where(level == lv, _dot_nt(lhs[h], rhs[h]).astype(BF16), a) for h, a in zip(halves, acc)]

    a_diag = pick(HG_LEVELS, q.astype(BF16), k.astype(BF16), [jnp.zeros((hc, hc), BF16)] * 2)
    row, col = f, jnp.ones_like(f)
    n = 1
    for lv in range(HG_LEVELS):
        a_diag = pick(lv, (q * row).astype(BF16), (k * col).astype(BF16), a_diag)
        row, col = _hgrn_scales_step(row, col, n, reverse)
        n *= 2

    qs, ks = (q * row).astype(BF16), (k * col).astype(BF16)
    lo, hi = halves
    if not reverse:
        a_x = _dot_nt(qs[hi], ks[lo]).astype(BF16)
        o_lo = _dot(a_diag[0], v_bf[lo])
        o_hi = _dot(jnp.concatenate([a_x, a_diag[1]], axis=1), v_bf)
    else:
        a_x = _dot_nt(qs[lo], ks[hi]).astype(BF16)
        o_lo = _dot(jnp.concatenate([a_diag[0], a_x], axis=1), v_bf)
        o_hi = _dot(a_diag[1], v_bf[hi])
    row, col = _hgrn_scales_step(row, col, n, reverse)

    st = state_ref[...]
    o = jnp.concatenate([o_lo, o_hi], axis=0) + _dot_nt((q * row).astype(BF16), st.astype(BF16))
    total = row[C - 1:C, :] if not reverse else row[0:1, :]
    state_ref[...] = st * total + _dot_tn(v_bf, (k * col).astype(BF16))
    return o


def _hgrn_kernel(qf_ref, vf_ref, ff_ref, qb_ref, vb_ref, fb_ref, lbf_ref, lbb_ref, lvf_ref, lvb_ref,
                 of_ref, ob_ref, sf_ref, sb_ref):
    @pl.when(pl.program_id(2) == 0)
    def _():
        sf_ref[...] = jnp.zeros_like(sf_ref)
        sb_ref[...] = jnp.zeros_like(sb_ref)

    for h in range(HG_HPS):
        qf = jax.nn.silu(qf_ref[h])
        of_ref[h] = _hgrn_direction(qf, ff_ref[h], lbf_ref[h], vf_ref[h], sf_ref.at[h], lvf_ref[...], False)
        qb = jax.nn.silu(qb_ref[h])
        ob_ref[h] = _hgrn_direction(qb, fb_ref[h], lbb_ref[h], vb_ref[h], sb_ref.at[h], lvb_ref[...], True)


def _hgrn_level_tables():
    hc = HG_CHUNK // 2
    t = np.arange(hc)[:, None]
    s = np.arange(hc)[None, :]
    x = t ^ s
    hb = np.where(x > 0, np.floor(np.log2(np.maximum(x, 1))), HG_LEVELS).astype(np.float32)
    return jnp.asarray(np.where(t >= s, hb, -1.0), BF16), jnp.asarray(np.where(t <= s, hb, -1.0), BF16)


def _hgrn(hq, hi, hff, hfb, lbf, lbb, batch):
    H, T, _ = hq.shape
    C = HG_CHUNK
    nc = T // batch // C
    fwd = pl.BlockSpec((HG_HPS, C, HG_DIM), lambda b, h, c: (h, b * nc + c, 0))
    bwd = pl.BlockSpec((HG_HPS, C, HG_DIM), lambda b, h, c: (h, b * nc + nc - 1 - c, 0))
    lb_spec = pl.BlockSpec((HG_HPS, 1, HG_DIM), lambda b, h, c: (h, 0, 0))
    lv_spec = _resident((C // 2, C // 2))
    lvf, lvb = _hgrn_level_tables()
    out = jax.ShapeDtypeStruct((H, T, HG_DIM), F32)
    state = pltpu.VMEM((HG_HPS, HG_DIM, HG_DIM), F32)
    return pl.pallas_call(
        _hgrn_kernel,
        grid=(batch, H // HG_HPS, nc),
        in_specs=[fwd, fwd, fwd, bwd, bwd, bwd, lb_spec, lb_spec, lv_spec, lv_spec],
        out_specs=[fwd, bwd],
        out_shape=[out, out],
        scratch_shapes=[state, state],
        compiler_params=pltpu.CompilerParams(
            dimension_semantics=("parallel", "parallel", "arbitrary"), vmem_limit_bytes=VMEM_LIMIT),
        name="hgrn",
    )(hq, hi, hff, hq, hi, hfb, lbf, lbb, lvf, lvb)


def _merge_kernel(x_ref, ge_ref, be_ref, yna_ref, of_ref, ob_ref, hog_ref, ng_ref, ymem_ref,
                  wg0_ref, wg1_ref, wna_ref, whg_ref, wmem_ref, wout_ref, g1_ref, b1_ref, o_ref):
    wg_refs = (wg0_ref, wg1_ref)

    def sub_tile(rows):
        xn = _layer_norm(x_ref[rows, :], ge_ref[...], be_ref[...])
        xb = xn.astype(BF16)
        yield
        pre = jnp.concatenate([_dot(xb, w[...]) for w in wg_refs], axis=-1)
        yield
        heads = []
        for h in range(HG_HEADS):
            o = of_ref[h, rows, :] + ob_ref[h, rows, :]
            heads.append(o * lax.rsqrt(jnp.mean(o * o, axis=-1, keepdims=True) + RMS_EPS))
        y_hg = jnp.concatenate(heads, axis=-1) * ng_ref[...] * jax.nn.silu(hog_ref[rows, :])
        yield
        branches = (_dot(yna_ref[rows, :], wna_ref[...]), _dot(y_hg.astype(BF16), whg_ref[...]),
                    _dot(ymem_ref[rows, :], wmem_ref[...]))
        yield
        gated = [jax.nn.sigmoid(pre[:, j * D_MODEL:(j + 1) * D_MODEL]) * br for j, br in enumerate(branches)]
        merged = gated[0] + gated[1] + gated[2]
        yield
        y = ALPHA * xn + _dot(merged.astype(BF16), wout_ref[...])
        yield
        o_ref[rows, :] = _layer_norm(y, g1_ref[...], b1_ref[...])

    tiles = _sub_tiles(x_ref.shape[0])
    for i in range(0, len(tiles), 2):
        _interleave([sub_tile(rows) for rows in tiles[i:i + 2]])


def _merge(x2, ge, be, y_na, o_f, o_b, hog, ng, y_mem, wg0, wg1, wna, whg, wmem, wout, g1, b1):
    T = x2.shape[0]
    tm = TOKEN_TILE
    row = lambda n: pl.BlockSpec((tm, n), lambda i: (i, 0))
    vec = lambda n: _resident((1, n))
    head = pl.BlockSpec((HG_HEADS, tm, HG_DIM), lambda i: (0, i, 0))
    assert wg0.shape[1] + wg1.shape[1] == 3 * D_MODEL
    return pl.pallas_call(
        _merge_kernel,
        grid=(T // tm,),
        in_specs=[
            row(D_MODEL), vec(D_MODEL), vec(D_MODEL),
            row(NA_WIDTH), head, head, row(HG_WIDTH), vec(HG_WIDTH), row(MEM_WIDTH),
            _resident(wg0.shape), _resident(wg1.shape), _resident((NA_WIDTH, D_MODEL)),
            _resident((HG_WIDTH, D_MODEL)),
            _resident((MEM_WIDTH, D_MODEL)), _resident((D_MODEL, D_MODEL)), vec(D_MODEL), vec(D_MODEL),
        ],
        out_specs=row(D_MODEL),
        out_shape=jax.ShapeDtypeStruct((T, D_MODEL), F32),
        compiler_params=pltpu.CompilerParams(
            dimension_semantics=("parallel",), vmem_limit_bytes=VMEM_LIMIT),
        name="merge",
    )(x2, ge, be, y_na, o_f, o_b, hog, ng, y_mem, wg0, wg1, wna, whg, wmem, wout, g1, b1)


def _ffn_kernel(x_ref, w1_ref, w2_ref, g_ref, b_ref, o_ref):
    def sub_tile(rows):
        x = x_ref[rows, :]
        xb = x.astype(BF16)
        acc = ALPHA * x
        for j in range(D_FF // D_MODEL):
            sl = slice(j * D_MODEL, (j + 1) * D_MODEL)
            h = jnp.maximum(_dot(xb, w1_ref[:, sl]), 0.0)
            yield
            acc = acc + _dot((h * h).astype(BF16), w2_ref[sl, :])
            yield
        o_ref[rows, :] = _layer_norm(acc, g_ref[...], b_ref[...])

    _interleave([sub_tile(rows) for rows in _sub_tiles(x_ref.shape[0])])


def _ffn(x1, w1, w2, g, b):
    T = x1.shape[0]
    tm = TOKEN_TILE
    return pl.pallas_call(
        _ffn_kernel,
        grid=(T // tm,),
        in_specs=[
            pl.BlockSpec((tm, D_MODEL), lambda i: (i, 0)),
            _resident((D_MODEL, D_FF)), _resident((D_FF, D_MODEL)),
            _resident((1, D_MODEL)), _resident((1, D_MODEL)),
        ],
        out_specs=pl.BlockSpec((tm, D_MODEL), lambda i: (i, 0)),
        out_shape=jax.ShapeDtypeStruct((T, D_MODEL), F32),
        compiler_params=pltpu.CompilerParams(
            dimension_semantics=("parallel",), vmem_limit_bytes=VMEM_LIMIT),
        name="ffn",
    )(x1, w1, w2, g, b)


def kernel(x, mem, ln_emb_g, ln_emb_b, w_in, na_rpb, hg_lb_logits, hg_norm_g, w_mem_kv, w_branch_na,
           w_branch_hg, w_branch_mem, w_out, ln1_g, ln1_b, w_ff1, w_ff2, ln2_g, ln2_b):
    B, S, D = x.shape
    assert D == D_MODEL and S % HG_CHUNK == 0 and S % GRID_W == 0 and S // GRID_W >= NA_KH
    assert w_in.shape[0] == DEPTH and (B * S) % TOKEN_TILE == 0
    T = B * S
    l = 0
    vec = lambda a: a.reshape(1, -1).astype(F32)

    w_br = w_in[l, :, :BRANCH_COLS].astype(BF16)
    lb_all = jnp.cumsum(jax.nn.softmax(hg_lb_logits.astype(F32), axis=1), axis=1)
    lbf = lb_all[0, l].reshape(HG_HEADS, 1, HG_DIM)
    lbb = lb_all[1, l].reshape(HG_HEADS, 1, HG_DIM)
    bands = _na_bias_bands(na_rpb[l])

    x2 = x.reshape(T, D)
    ge, be = vec(ln_emb_g), vec(ln_emb_b)
    qkv, hq, hi, hog, hff, hfb, y_mem = _ln_proj(x2, ge, be, w_br, mem, w_mem_kv[l].astype(BF16), S)
    casts = [(w_in[l], BRANCH_COLS, W_IN_BLOCK), (w_in[l], BRANCH_COLS + W_IN_BLOCK, W_IN_BLOCK),
             (w_branch_na[l], 0, D_MODEL), (w_branch_hg[l], 0, D_MODEL), (w_branch_mem[l], 0, D_MODEL),
             (w_out[l], 0, D_MODEL), (w_ff1[l], 0, D_FF), (w_ff2[l], 0, D_MODEL)]
    y_na, (wg0, wg1, wna, whg, wmem, wout, w1, w2) = _na_attn(qkv.reshape(B, S, 3 * NA_WIDTH), bands, casts)
    o_f, o_b = _hgrn(hq, hi, hff, hfb, lbf, lbb, B)
    x1 = _merge(x2, ge, be, y_na.reshape(T, NA_WIDTH), o_f, o_b, hog, vec(hg_norm_g[l]), y_mem,
                wg0, wg1, wna, whg, wmem, wout, vec(ln1_g[l]), vec(ln1_b[l]))
    out = _ffn(x1, w1, w2, vec(ln2_g[l]), vec(ln2_b[l]))
    return out.reshape(B, S, D)
```

```python
import functools

import numpy as np
import jax
import jax.numpy as jnp
from jax import lax
from jax.experimental import pallas as pl
from jax.experimental.pallas import tpu as pltpu

F32 = jnp.float32
BF16 = jnp.bfloat16

D_MODEL = 1024
GRID_W = 64
NA_HEADS = 8
NA_HEAD_DIM = 64
NA_WIDTH = NA_HEADS * NA_HEAD_DIM
NA_KH = 8
NA_KW = 16
HG_HEADS = 4
HG_DIM = 128
HG_WIDTH = HG_HEADS * HG_DIM
MEM_HEADS = 4
MEM_HEAD_DIM = 128
MEM_WIDTH = MEM_HEADS * MEM_HEAD_DIM
D_FF = 4 * D_MODEL
DEPTH = 1
ALPHA = (2.0 * DEPTH) ** 0.25
LN_EPS = 1e-5
RMS_EPS = 1e-6
BRANCH_COLS = 9 * 512
W_IN_BLOCK = 1536
MASK_NEG = -1e30

LANES = 128
SUBLANES = 8
VMEM_LIMIT = 56 * 1024 * 1024
TOKEN_TILE = 1024
SUB_TILE = 256
HG_CHUNK = 256
HG_HPS = 4
HG_LEVELS = 7
NA_GROUP = 4
NA_WIN = 12
NA_PAIRS = 22
NA_UNROLL = 8


def _layer_norm(x, g, b):
    mu = jnp.mean(x, axis=-1, keepdims=True)
    xc = x - mu
    var = jnp.mean(xc * xc, axis=-1, keepdims=True)
    return xc * lax.rsqrt(var + LN_EPS) * g + b


def _sub_tiles(tm):
    return [slice(r, r + SUB_TILE) for r in range(0, tm, SUB_TILE)]


def _interleave(chains, skew=1):
    chains = list(chains)
    start = {id(c): skew * i for i, c in enumerate(chains)}
    rnd = 0
    while chains:
        for chain in list(chains):
            if rnd >= start[id(chain)] and next(chain, StopIteration) is StopIteration:
                chains.remove(chain)
        rnd += 1


def _resident(shape):
    return pl.BlockSpec(shape, lambda *_: (0,) * len(shape), pipeline_mode=pl.Buffered(1))


def _dot(a, b):
    return jnp.dot(a, b, preferred_element_type=F32)


def _dot_nt(a, b):
    return lax.dot_general(a, b, (((1,), (1,)), ((), ())), preferred_element_type=F32)


def _dot_tn(a, b):
    return lax.dot_general(a, b, (((0,), (0,)), ((), ())), preferred_element_type=F32)


def _w_in_cols(w_refs, first_block, c0, n):
    blk, off = divmod(c0 - first_block * W_IN_BLOCK, W_IN_BLOCK)
    assert off + n <= W_IN_BLOCK
    return w_refs[blk][:, off:off + n]


def _ln_proj_kernel(x_ref, g_ref, b_ref, w0_ref, w1_ref, w2_ref, mem_ref, wkv_ref,
                    qkv_ref, hq_ref, hi_ref, hog_ref, hff_ref, hfb_ref, ymem_ref, k_scr, v_scr, *, tiles_per_seq):
    @pl.when(pl.program_id(0) % tiles_per_seq == 0)
    def _():
        mb = mem_ref[0].astype(BF16)
        k_scr[...] = _dot(mb, wkv_ref[:, :MEM_WIDTH]).astype(BF16)
        v_scr[...] = _dot(mb, wkv_ref[:, MEM_WIDTH:]).astype(BF16)

    w_refs = (w0_ref, w1_ref, w2_ref)
    mem_scale = MEM_HEAD_DIM ** -0.5
    head_sl = [slice(h * MEM_HEAD_DIM, (h + 1) * MEM_HEAD_DIM) for h in range(MEM_HEADS)]

    def sub_tile(rows):
        xb = _layer_norm(x_ref[rows, :], g_ref[...], b_ref[...]).astype(BF16)
        yield

        def cols(j):
            return _dot(xb, _w_in_cols(w_refs, 0, j * 512, 512))

        def heads_out(j, ref):
            r = cols(j)
            for h in range(HG_HEADS):
                ref[h, rows, :] = r[:, h * HG_DIM:(h + 1) * HG_DIM].astype(ref.dtype)

        mq = cols(8).astype(BF16)
        scores = [_dot_nt(mq[:, sl], k_scr[:, sl]) * mem_scale for sl in head_sl]
        for j in range(3):
            qkv_ref[rows, j * 512:(j + 1) * 512] = cols(j).astype(BF16)
        yield

        probs = []
        for j, ref, heads in ((3, hq_ref, (0, 1)), (4, hi_ref, (2, 3))):
            for h in heads:
                p = jnp.exp(scores[h] - jnp.max(scores[h], axis=-1, keepdims=True))
                probs.append((p.astype(BF16), jnp.sum(p, axis=-1, keepdims=True)))
            yield
            heads_out(j, ref)
            yield

        hog_ref[rows, :] = cols(5)
        heads_out(6, hff_ref)
        heads_out(7, hfb_ref)
        for sl, (p, l) in zip(head_sl, probs):
            ymem_ref[rows, sl] = (_dot(p, v_scr[:, sl]) / l).astype(ymem_ref.dtype)

    _interleave([sub_tile(rows) for rows in _sub_tiles(x_ref.shape[0])])


def _ln_proj(x2, g, b, w_in_b, mem, wkv, seq):
    T = x2.shape[0]
    tm = TOKEN_TILE
    assert seq % tm == 0 and BRANCH_COLS == 3 * W_IN_BLOCK
    tiles_per_seq = seq // tm
    M = mem.shape[1]
    head_spec = pl.BlockSpec((HG_HEADS, tm, HG_DIM), lambda i: (0, i, 0))
    head_f32 = jax.ShapeDtypeStruct((HG_HEADS, T, HG_DIM), F32)
    w_blk = lambda j: pl.BlockSpec((D_MODEL, W_IN_BLOCK), lambda i: (0, j), pipeline_mode=pl.Buffered(1))
    return pl.pallas_call(
        functools.partial(_ln_proj_kernel, tiles_per_seq=tiles_per_seq),
        grid=(T // tm,),
        in_specs=[
            pl.BlockSpec((tm, D_MODEL), lambda i: (i, 0)),
            _resident((1, D_MODEL)), _resident((1, D_MODEL)), w_blk(0), w_blk(1), w_blk(2),
            pl.BlockSpec((1, M, D_MODEL), lambda i: (i // tiles_per_seq, 0, 0)),
            _resident((D_MODEL, 2 * MEM_WIDTH)),
        ],
        out_specs=[
            pl.BlockSpec((tm, 3 * NA_WIDTH), lambda i: (i, 0)),
            head_spec, head_spec,
            pl.BlockSpec((tm, HG_WIDTH), lambda i: (i, 0)),
            head_spec, head_spec,
            pl.BlockSpec((tm, MEM_WIDTH), lambda i: (i, 0)),
        ],
        out_shape=[
            jax.ShapeDtypeStruct((T, 3 * NA_WIDTH), BF16),
            head_f32,
            jax.ShapeDtypeStruct((HG_HEADS, T, HG_DIM), BF16),
            jax.ShapeDtypeStruct((T, HG_WIDTH), F32),
            head_f32, head_f32,
            jax.ShapeDtypeStruct((T, MEM_WIDTH), BF16),
        ],
        scratch_shapes=[pltpu.VMEM((M, MEM_WIDTH), BF16), pltpu.VMEM((M, MEM_WIDTH), BF16)],
        compiler_params=pltpu.CompilerParams(
            dimension_semantics=("arbitrary",), vmem_limit_bytes=VMEM_LIMIT),
        name="ln_proj",
    )(x2, g, b, w_in_b, w_in_b, w_in_b, mem, wkv)


def _na_bias_bands(rpb):
    qc = np.arange(GRID_W)[None, :]
    kc = np.arange(GRID_W)[:, None]
    c0 = np.clip(qc - NA_KW // 2, 0, GRID_W - NA_KW)
    valid = (kc >= c0) & (kc < c0 + NA_KW)
    dc = kc - qc + NA_KW - 1
    onehot = ((dc[None] == np.arange(2 * NA_KW - 1)[:, None, None]) & valid[None]).astype(np.float32)
    bnd = jnp.einsum("hdj,jkq->hdkq", rpb.astype(F32), jnp.asarray(onehot), precision=lax.Precision.HIGHEST)
    bnd = bnd + jnp.asarray(np.where(valid, 0.0, MASK_NEG).astype(np.float32))
    pad = jnp.full((NA_HEADS, 4, GRID_W, GRID_W), MASK_NEG, F32)
    return jnp.concatenate([pad, bnd, pad], axis=1)


def _na_kernel(q_ref, k_ref, v_ref, band_ref, *rest, rows, n_cast):
    cast_in, o_ref, cast_out, tab_ref = rest[:n_cast], rest[n_cast], rest[n_cast + 1:-1], rest[-1]
    for src, dst in zip(cast_in, cast_out):
        dst[...] = src[...].astype(dst.dtype)

    @pl.when(pl.program_id(1) == 0)
    def _():
        masked = jnp.full((GRID_W, GRID_W), MASK_NEG, F32)
        cat = lambda a, b: jnp.concatenate([a, b], axis=-1)
        for h in range(2):
            for i in range(NA_PAIRS):
                left, right = band_ref[h, i + 1], band_ref[h, i]
                tab_ref[h, i] = cat(left, right)
                tab_ref[h, NA_PAIRS + i] = cat(left, masked)
                tab_ref[h, 2 * NA_PAIRS + i] = cat(masked, right)
            tab_ref[h, 3 * NA_PAIRS] = cat(masked, masked)

    scale = NA_HEAD_DIM ** -0.5
    assert np.log2(scale) == np.round(np.log2(scale))
    gq = NA_GROUP * GRID_W
    first = lax.broadcasted_iota(jnp.int32, (gq, LANES), 1) < NA_HEAD_DIM

    def window_start(g):
        return int(np.clip(g * NA_GROUP - NA_KH // 2, 0, rows - NA_WIN))

    def scores(g):
        r0, ws = g * NA_GROUP, window_start(g)
        q = q_ref[0, r0 * GRID_W:r0 * GRID_W + gq, :] * jnp.asarray(scale, BF16)
        kw = k_ref[0, ws * GRID_W:(ws + NA_WIN) * GRID_W, :]
        zero = jnp.zeros_like(q)
        qs = jnp.concatenate([jnp.where(first, q, zero), jnp.where(first, zero, q)], axis=0)
        return _dot_nt(kw, qs)

    def table_index(g, jp, a):
        r, kr = g * NA_GROUP + 2 * jp, window_start(g) + a
        in_window = lambda row: 0 <= kr - int(np.clip(row - NA_KH // 2, 0, rows - NA_KH)) < NA_KH
        lv, rv = in_window(r), in_window(r + 1)
        i = kr - r + (NA_KH - 1) + 3
        return i if lv and rv else NA_PAIRS + i if lv else 2 * NA_PAIRS + i if rv else 3 * NA_PAIRS

    def softmax(g, s_t):
        ps, inv_ls = [], []
        for h in range(2):
            for jp in range(NA_GROUP // 2):
                live = [a for a in range(NA_WIN) if table_index(g, jp, a) != 3 * NA_PAIRS]
                a0, a1 = live[0], live[-1] + 1
                assert live == list(range(a0, a1))
                bias = jnp.concatenate([tab_ref[h, table_index(g, jp, a)] for a in range(a0, a1)], axis=0)
                lo = (h * (NA_GROUP // 2) + jp) * LANES
                s = (s_t[a0 * GRID_W:a1 * GRID_W, lo:lo + LANES] + bias).reshape(a1 - a0, GRID_W, LANES)
                m = jnp.max(jnp.max(s, axis=0), axis=0, keepdims=True)
                p = jnp.exp(s - m)
                inv_ls.append(1.0 / jnp.sum(jnp.sum(p, axis=0), axis=0, keepdims=True))
                blocks = [jnp.zeros((a0 * GRID_W, LANES), BF16),
                          p.astype(BF16).reshape((a1 - a0) * GRID_W, LANES),
                          jnp.zeros(((NA_WIN - a1) * GRID_W, LANES), BF16)]
                ps.append(jnp.concatenate([b for b in blocks if b.shape[0]], axis=0))
        return jnp.concatenate(ps, axis=1), jnp.concatenate(inv_ls, axis=1)

    def output(g, p_t, inv_l):
        r0, ws = g * NA_GROUP, window_start(g)
        vw = v_ref[0, ws * GRID_W:(ws + NA_WIN) * GRID_W, :]
        o_t = _dot_tn(vw, p_t) * inv_l
        o = jnp.concatenate([o_t[:NA_HEAD_DIM, :gq], o_t[NA_HEAD_DIM:, gq:]], axis=0).T
        o_ref[0, r0 * GRID_W:r0 * GRID_W + gq, :] = o.astype(o_ref.dtype)

    n_groups = rows // NA_GROUP
    s_next = scores(0)
    for g in range(n_groups):
        s_cur = s_next
        if g + 1 < n_groups:
            s_next = scores(g + 1)
        output(g, *softmax(g, s_cur))


def _na_attn(qkv3, bands, casts):
    B, S, _ = qkv3.shape
    rows = S // GRID_W
    assert rows % NA_GROUP == 0 and rows >= NA_WIN and bands.shape[1] == NA_PAIRS + 1
    n_pairs = NA_HEADS // 2
    steps = n_pairs * B
    blk = lambda c0: pl.BlockSpec((1, S, LANES), lambda hp, b: (b, 0, c0 + hp))
    cast_in, cast_out, cast_shape = [], [], []
    for w, c0, nc in casts:
        slab = w.shape[0] // steps
        assert slab * steps == w.shape[0] and slab % 16 == 0 and c0 % nc == 0
        cast_in.append(pl.BlockSpec((slab, nc), lambda hp, b, j=c0 // nc: (hp * B + b, j)))
        cast_out.append(pl.BlockSpec((slab, nc), lambda hp, b: (hp * B + b, 0)))
        cast_shape.append(jax.ShapeDtypeStruct((w.shape[0], nc), BF16))
    out = pl.pallas_call(
        functools.partial(_na_kernel, rows=rows, n_cast=len(casts)),
        grid=(n_pairs, B),
        in_specs=[
            blk(0), blk(n_pairs), blk(2 * n_pairs),
            pl.BlockSpec((2, NA_PAIRS + 1, GRID_W, GRID_W), lambda hp, b: (hp, 0, 0, 0)),
        ] + cast_in,
        out_specs=[pl.BlockSpec((1, S, LANES), lambda hp, b: (b, 0, hp))] + cast_out,
        out_shape=[jax.ShapeDtypeStruct((B, S, NA_WIDTH), BF16)] + cast_shape,
        scratch_shapes=[pltpu.VMEM((2, 3 * NA_PAIRS + 1, GRID_W, LANES), F32)],
        compiler_params=pltpu.CompilerParams(
            dimension_semantics=("arbitrary", "arbitrary"), vmem_limit_bytes=VMEM_LIMIT),
        name="na_attn",
    )(qkv3, qkv3, qkv3, bands, *[w for w, _, _ in casts])
    return out[0], out[1:]


def _hgrn_scales_step(row, col, n, reverse):
    C = row.shape[0]
    nv = C // SUBLANES
    if n < SUBLANES:
        r3 = row.reshape(nv, SUBLANES, LANES)
        c3 = col.reshape(nv, SUBLANES, LANES)
        sub = lax.broadcasted_iota(jnp.int32, (nv, SUBLANES, LANES), 1)
        in_g = (sub & n) != 0
        bc = lambda i: jnp.broadcast_to(r3[:, i:i + 1, :], r3.shape)
        if n == 1:
            up = pltpu.roll(r3, 1, axis=1)
            dn = pltpu.roll(r3, SUBLANES - 1, axis=1)
            if not reverse:
                new_r = jnp.where(in_g, r3 * up, r3)
                new_c = jnp.where(in_g, c3, c3 * dn)
            else:
                new_r = jnp.where(in_g, r3, r3 * dn)
                new_c = jnp.where(in_g, c3 * up, c3)
        else:
            lo = sub < 4
            if not reverse:
                if n == 2:
                    tf = jnp.where(lo, bc(1), bc(5))
                    tg = jnp.where(lo, bc(3), bc(7))
                else:
                    tf, tg = bc(3), bc(7)
            else:
                if n == 2:
                    tf = jnp.where(lo, bc(0), bc(4))
                    tg = jnp.where(lo, bc(2), bc(6))
                else:
                    tf, tg = bc(0), bc(4)
            if not reverse:
                new_r = jnp.where(in_g, r3 * tf, r3)
                new_c = jnp.where(in_g, c3, c3 * tg)
            else:
                new_r = jnp.where(in_g, r3, r3 * tg)
                new_c = jnp.where(in_g, c3 * tf, c3)
        return new_r.reshape(C, LANES), new_c.reshape(C, LANES)

    new_r, new_c = [], []
    for base in range(0, C, 2 * n):
        f_sl, g_sl = slice(base, base + n), slice(base + n, base + 2 * n)
        if not reverse:
            tf, tg = row[base + n - 1:base + n], row[base + 2 * n - 1:base + 2 * n]
            new_r += [row[f_sl], row[g_sl] * tf]
            new_c += [col[f_sl] * tg, col[g_sl]]
        else:
            tf, tg = row[base:base + 1], row[base + n:base + n + 1]
            new_r += [row[f_sl] * tg, row[g_sl]]
            new_c += [col[f_sl], col[g_sl] * tf]
    return jnp.concatenate(new_r, axis=0), jnp.concatenate(new_c, axis=0)


def _hgrn_direction(q, fpre, lb, v_bf, state_ref, level, reverse):
    C = q.shape[0]
    hc = C // 2
    halves = (slice(0, hc), slice(hc, C))
    f = lb + (1.0 - lb) * jax.nn.sigmoid(fpre)
    k = 1.0 - f

    def pick(lv, lhs, rhs, acc):
        return [jnp.where(level == lv, _dot(lhs[h].astype(BF16), rhs[h].T.astype(BF16)).astype(BF16), a)
                for h, a in zip(halves, acc)]

    a_diag = pick(HG_LEVELS, q, k, [jnp.zeros((hc, hc), BF16)] * 2)
    row, col = f, jnp.ones_like(f)
    n = 1
    for lv in range(HG_LEVELS):
        a_diag = pick(lv, q * row, k * col, a_diag)
        row, col = _hgrn_scales_step(row, col, n, reverse)
        n *= 2

    qs, ks = (q * row).astype(BF16), k * col
    lo, hi = halves
    if not reverse:
        a_x = _dot(qs[hi], ks[lo].T.astype(BF16)).astype(BF16)
        o_lo = _dot(a_diag[0], v_bf[lo])
        o_hi = _dot(jnp.concatenate([a_x, a_diag[1]], axis=1), v_bf)
    else:
        a_x = _dot(qs[lo], ks[hi].T.astype(BF16)).astype(BF16)
        o_lo = _dot(jnp.concatenate([a_diag[0], a_x], axis=1), v_bf)
        o_hi = _dot(a_diag[1], v_bf[hi])
    row, col = _hgrn_scales_step(row, col, n, reverse)

    st = state_ref[...]
    o = jnp.concatenate([o_lo, o_hi], axis=0) + _dot_nt((q * row).astype(BF16), st.astype(BF16))
    total = row[C - 1:C, :] if not reverse else row[0:1, :]
    state_ref[...] = st * total + _dot_tn(v_bf, (k * col).astype(BF16))
    return o


def _hgrn_kernel(qf_ref, vf_ref, ff_ref, qb_ref, vb_ref, fb_ref, lbf_ref, lbb_ref, lvf_ref, lvb_ref,
                 of_ref, ob_ref, sf_ref, sb_ref):
    @pl.when(pl.program_id(2) == 0)
    def _():
        sf_ref[...] = jnp.zeros_like(sf_ref)
        sb_ref[...] = jnp.zeros_like(sb_ref)

    for h in range(HG_HPS):
        qf = jax.nn.silu(qf_ref[h])
        of_ref[h] = _hgrn_direction(qf, ff_ref[h], lbf_ref[h], vf_ref[h], sf_ref.at[h], lvf_ref[...], False)
        qb = jax.nn.silu(qb_ref[h])
        ob_ref[h] = _hgrn_direction(qb, fb_ref[h], lbb_ref[h], vb_ref[h], sb_ref.at[h], lvb_ref[...], True)


def _hgrn_level_tables():
    hc = HG_CHUNK // 2
    t = np.arange(hc)[:, None]
    s = np.arange(hc)[None, :]
    x = t ^ s
    hb = np.where(x > 0, np.floor(np.log2(np.maximum(x, 1))), HG_LEVELS).astype(np.float32)
    return jnp.asarray(np.where(t >= s, hb, -1.0), BF16), jnp.asarray(np.where(t <= s, hb, -1.0), BF16)


def _hgrn(hq, hi, hff, hfb, lbf, lbb, batch):
    H, T, _ = hq.shape
    C = HG_CHUNK
    nc = T // batch // C
    fwd = pl.BlockSpec((HG_HPS, C, HG_DIM), lambda b, h, c: (h, b * nc + c, 0))
    bwd = pl.BlockSpec((HG_HPS, C, HG_DIM), lambda b, h, c: (h, b * nc + nc - 1 - c, 0))
    lb_spec = pl.BlockSpec((HG_HPS, 1, HG_DIM), lambda b, h, c: (h, 0, 0))
    lv_spec = _resident((C // 2, C // 2))
    lvf, lvb = _hgrn_level_tables()
    out = jax.ShapeDtypeStruct((H, T, HG_DIM), F32)
    state = pltpu.VMEM((HG_HPS, HG_DIM, HG_DIM), F32)
    return pl.pallas_call(
        _hgrn_kernel,
        grid=(batch, H // HG_HPS, nc),
        in_specs=[fwd, fwd, fwd, bwd, bwd, bwd, lb_spec, lb_spec, lv_spec, lv_spec],
        out_specs=[fwd, bwd],
        out_shape=[out, out],
        scratch_shapes=[state, state],
        compiler_params=pltpu.CompilerParams(
            dimension_semantics=("parallel", "parallel", "arbitrary"), vmem_limit_bytes=VMEM_LIMIT),
        name="hgrn",
    )(hq, hi, hff, hq, hi, hfb, lbf, lbb, lvf, lvb)


def _merge_kernel(x_ref, ge_ref, be_ref, yna_ref, of_ref, ob_ref, hog_ref, ng_ref, ymem_ref,
                  wg0_ref, wg1_ref, wna_ref, whg_ref, wmem_ref, wout_ref, g1_ref, b1_ref, o_ref):
    wg_refs = (wg0_ref, wg1_ref)

    def sub_tile(rows):
        xn = _layer_norm(x_ref[rows, :], ge_ref[...], be_ref[...])
        xb = xn.astype(BF16)
        yield
        pre = jnp.concatenate([_dot(xb, w[...]) for w in wg_refs], axis=-1)
        yield
        heads = []
        for h in range(HG_HEADS):
            o = of_ref[h, rows, :] + ob_ref[h, rows, :]
            heads.append(o * lax.rsqrt(jnp.mean(o * o, axis=-1, keepdims=True) + RMS_EPS))
        y_hg = jnp.concatenate(heads, axis=-1) * ng_ref[...] * jax.nn.silu(hog_ref[rows, :])
        yield
        branches = (_dot(yna_ref[rows, :], wna_ref[...]), _dot(y_hg.astype(BF16), whg_ref[...]),
                    _dot(ymem_ref[rows, :], wmem_ref[...]))
        yield
        gated = [jax.nn.sigmoid(pre[:, j * D_MODEL:(j + 1) * D_MODEL]) * br for j, br in enumerate(branches)]
        merged = gated[0] + gated[1] + gated[2]
        yield
        y = ALPHA * xn + _dot(merged.astype(BF16), wout_ref[...])
        yield
        o_ref[rows, :] = _layer_norm(y, g1_ref[...], b1_ref[...])

    tiles = _sub_tiles(x_ref.shape[0])
    for i in range(0, len(tiles), 2):
        _interleave([sub_tile(rows) for rows in tiles[i:i + 2]])


def _merge(x2, ge, be, y_na, o_f, o_b, hog, ng, y_mem, wg0, wg1, wna, whg, wmem, wout, g1, b1):
    T = x2.shape[0]
    tm = TOKEN_TILE
    row = lambda n: pl.BlockSpec((tm, n), lambda i: (i, 0))
    vec = lambda n: _resident((1, n))
    head = pl.BlockSpec((HG_HEADS, tm, HG_DIM), lambda i: (0, i, 0))
    assert wg0.shape[1] + wg1.shape[1] == 3 * D_MODEL
    return pl.pallas_call(
        _merge_kernel,
        grid=(T // tm,),
        in_specs=[
            row(D_MODEL), vec(D_MODEL), vec(D_MODEL),
            row(NA_WIDTH), head, head, row(HG_WIDTH), vec(HG_WIDTH), row(MEM_WIDTH),
            _resident(wg0.shape), _resident(wg1.shape), _resident((NA_WIDTH, D_MODEL)),
            _resident((HG_WIDTH, D_MODEL)),
            _resident((MEM_WIDTH, D_MODEL)), _resident((D_MODEL, D_MODEL)), vec(D_MODEL), vec(D_MODEL),
        ],
        out_specs=row(D_MODEL),
        out_shape=jax.ShapeDtypeStruct((T, D_MODEL), F32),
        compiler_params=pltpu.CompilerParams(
            dimension_semantics=("parallel",), vmem_limit_bytes=VMEM_LIMIT),
        name="merge",
    )(x2, ge, be, y_na, o_f, o_b, hog, ng, y_mem, wg0, wg1, wna, whg, wmem, wout, g1, b1)


def _ffn_kernel(x_ref, w1_ref, w2_ref, g_ref, b_ref, o_ref):
    def sub_tile(rows):
        x = x_ref[rows, :]
        xb = x.astype(BF16)
        acc = ALPHA * x
        for j in range(D_FF // D_MODEL):
            sl = slice(j * D_MODEL, (j + 1) * D_MODEL)
            h = jnp.maximum(_dot(xb, w1_ref[:, sl]), 0.0)
            yield
            acc = acc + _dot((h * h).astype(BF16), w2_ref[sl, :])
            yield
        o_ref[rows, :] = _layer_norm(acc, g_ref[...], b_ref[...])

    _interleave([sub_tile(rows) for rows in _sub_tiles(x_ref.shape[0])])


def _ffn(x1, w1, w2, g, b):
    T = x1.shape[0]
    tm = TOKEN_TILE
    return pl.pallas_call(
        _ffn_kernel,
        grid=(T // tm,),
        in_specs=[
            pl.BlockSpec((tm, D_MODEL), lambda i: (i, 0)),
            _resident((D_MODEL, D_FF)), _resident((D_FF, D_MODEL)),
            _resident((1, D_MODEL)), _resident((1, D_MODEL)),
        ],
        out_specs=pl.BlockSpec((tm, D_MODEL), lambda i: (i, 0)),
        out_shape=jax.ShapeDtypeStruct((T, D_MODEL), F32),
        compiler_params=pltpu.CompilerParams(
            dimension_semantics=("parallel",), vmem_limit_bytes=VMEM_LIMIT),
        name="ffn",
    )(x1, w1, w2, g, b)


def kernel(x, mem, ln_emb_g, ln_emb_b, w_in, na_rpb, hg_lb_logits, hg_norm_g, w_mem_kv, w_branch_na,
           w_branch_hg, w_branch_mem, w_out, ln1_g, ln1_b, w_ff1, w_ff2, ln2_g, ln2_b):
    B, S, D = x.shape
    assert D == D_MODEL and S % HG_CHUNK == 0 and S % GRID_W == 0 and S // GRID_W >= NA_KH
    assert w_in.shape[0] == DEPTH and (B * S) % TOKEN_TILE == 0
    T = B * S
    l = 0
    vec = lambda a: a.reshape(1, -1).astype(F32)

    w_br = w_in[l, :, :BRANCH_COLS].astype(BF16)
    lb_all = jnp.cumsum(jax.nn.softmax(hg_lb_logits.astype(F32), axis=1), axis=1)
    lbf = lb_all[0, l].reshape(HG_HEADS, 1, HG_DIM)
    lbb = lb_all[1, l].reshape(HG_HEADS, 1, HG_DIM)
    bands = _na_bias_bands(na_rpb[l])

    x2 = x.reshape(T, D)
    ge, be = vec(ln_emb_g), vec(ln_emb_b)
    qkv, hq, hi, hog, hff, hfb, y_mem = _ln_proj(x2, ge, be, w_br, mem, w_mem_kv[l].astype(BF16), S)
    casts = [(w_in[l], BRANCH_COLS, W_IN_BLOCK), (w_in[l], BRANCH_COLS + W_IN_BLOCK, W_IN_BLOCK),
             (w_branch_na[l], 0, D_MODEL), (w_branch_hg[l], 0, D_MODEL), (w_branch_mem[l], 0, D_MODEL),
             (w_out[l], 0, D_MODEL), (w_ff1[l], 0, D_FF), (w_ff2[l], 0, D_MODEL)]
    y_na, (wg0, wg1, wna, whg, wmem, wout, w1, w2) = _na_attn(qkv.reshape(B, S, 3 * NA_WIDTH), bands, casts)
    o_f, o_b = _hgrn(hq, hi, hff, hfb, lbf, lbb, B)
    x1 = _merge(x2, ge, be, y_na.reshape(T, NA_WIDTH), o_f, o_b, hog, vec(hg_norm_g[l]), y_mem,
                wg0, wg1, wna, whg, wmem, wout, vec(ln1_g[l]), vec(ln1_b[l]))
    out = _ffn(x1, w1, w2, vec(ln2_g[l]), vec(ln2_b[l]))
    return out.reshape(B, S, D)
```

```python
import functools

import numpy as np
import jax
import jax.numpy as jnp
from jax import lax
from jax.experimental import pallas as pl
from jax.experimental.pallas import tpu as pltpu

F32 = jnp.float32
BF16 = jnp.bfloat16

D_MODEL = 1024
GRID_W = 64
NA_HEADS = 8
NA_HEAD_DIM = 64
NA_WIDTH = NA_HEADS * NA_HEAD_DIM
NA_KH = 8
NA_KW = 16
HG_HEADS = 4
HG_DIM = 128
HG_WIDTH = HG_HEADS * HG_DIM
MEM_HEADS = 4
MEM_HEAD_DIM = 128
MEM_WIDTH = MEM_HEADS * MEM_HEAD_DIM
D_FF = 4 * D_MODEL
DEPTH = 1
ALPHA = (2.0 * DEPTH) ** 0.25
LN_EPS = 1e-5
RMS_EPS = 1e-6
BRANCH_COLS = 9 * 512
W_IN_BLOCK = 1536
MASK_NEG = -1e30

LANES = 128
SUBLANES = 8
VMEM_LIMIT = 56 * 1024 * 1024
TOKEN_TILE = 1024
SUB_TILE = 256
HG_CHUNK = 256
HG_HPS = 4
HG_LEVELS = 7
NA_GROUP = 4
NA_WIN = 12
NA_PAIRS = 22
NA_UNROLL = 8


def _layer_norm(x, g, b):
    mu = jnp.mean(x, axis=-1, keepdims=True)
    xc = x - mu
    var = jnp.mean(xc * xc, axis=-1, keepdims=True)
    return xc * lax.rsqrt(var + LN_EPS) * g + b


def _sub_tiles(tm):
    return [slice(r, r + SUB_TILE) for r in range(0, tm, SUB_TILE)]


def _interleave(chains, skew=1):
    chains = list(chains)
    start = {id(c): skew * i for i, c in enumerate(chains)}
    rnd = 0
    while chains:
        for chain in list(chains):
            if rnd >= start[id(chain)] and next(chain, StopIteration) is StopIteration:
                chains.remove(chain)
        rnd += 1


def _resident(shape):
    return pl.BlockSpec(shape, lambda *_: (0,) * len(shape), pipeline_mode=pl.Buffered(1))


def _dot(a, b):
    return jnp.dot(a, b, preferred_element_type=F32)


def _dot_nt(a, b):
    return lax.dot_general(a, b, (((1,), (1,)), ((), ())), preferred_element_type=F32)


def _dot_tn(a, b):
    return lax.dot_general(a, b, (((0,), (0,)), ((), ())), preferred_element_type=F32)


def _w_in_cols(w_refs, first_block, c0, n):
    blk, off = divmod(c0 - first_block * W_IN_BLOCK, W_IN_BLOCK)
    assert off + n <= W_IN_BLOCK
    return w_refs[blk][:, off:off + n]


def _ln_proj_kernel(x_ref, g_ref, b_ref, w0_ref, w1_ref, w2_ref, mem_ref, wkv_ref,
                    qkv_ref, hq_ref, hi_ref, hog_ref, hff_ref, hfb_ref, ymem_ref, k_scr, v_scr, *, tiles_per_seq):
    @pl.when(pl.program_id(0) % tiles_per_seq == 0)
    def _():
        mb = mem_ref[0].astype(BF16)
        k_scr[...] = _dot(mb, wkv_ref[:, :MEM_WIDTH]).astype(BF16)
        v_scr[...] = _dot(mb, wkv_ref[:, MEM_WIDTH:]).astype(BF16)

    w_refs = (w0_ref, w1_ref, w2_ref)
    mem_scale = MEM_HEAD_DIM ** -0.5
    head_sl = [slice(h * MEM_HEAD_DIM, (h + 1) * MEM_HEAD_DIM) for h in range(MEM_HEADS)]

    def sub_tile(rows):
        xb = _layer_norm(x_ref[rows, :], g_ref[...], b_ref[...]).astype(BF16)
        yield

        def cols(j):
            return _dot(xb, _w_in_cols(w_refs, 0, j * 512, 512))

        def heads_out(j, ref):
            r = cols(j)
            for h in range(HG_HEADS):
                ref[h, rows, :] = r[:, h * HG_DIM:(h + 1) * HG_DIM].astype(ref.dtype)

        mq = cols(8).astype(BF16)
        scores = [_dot_nt(mq[:, sl], k_scr[:, sl]) * mem_scale for sl in head_sl]
        for j in range(3):
            qkv_ref[rows, j * 512:(j + 1) * 512] = cols(j).astype(BF16)
        yield

        probs = []
        for j, ref, heads in ((3, hq_ref, (0, 1)), (4, hi_ref, (2, 3))):
            for h in heads:
                p = jnp.exp(scores[h] - jnp.max(scores[h], axis=-1, keepdims=True))
                probs.append((p.astype(BF16), jnp.sum(p, axis=-1, keepdims=True)))
            yield
            heads_out(j, ref)
            yield

        hog_ref[rows, :] = cols(5)
        heads_out(6, hff_ref)
        heads_out(7, hfb_ref)
        for sl, (p, l) in zip(head_sl, probs):
            ymem_ref[rows, sl] = (_dot(p, v_scr[:, sl]) / l).astype(ymem_ref.dtype)

    _interleave([sub_tile(rows) for rows in _sub_tiles(x_ref.shape[0])])


def _ln_proj(x2, g, b, w_in_b, mem, wkv, seq):
    T = x2.shape[0]
    tm = TOKEN_TILE
    assert seq % tm == 0 and BRANCH_COLS == 3 * W_IN_BLOCK
    tiles_per_seq = seq // tm
    M = mem.shape[1]
    head_spec = pl.BlockSpec((HG_HEADS, tm, HG_DIM), lambda i: (0, i, 0))
    head_f32 = jax.ShapeDtypeStruct((HG_HEADS, T, HG_DIM), F32)
    w_blk = lambda j: pl.BlockSpec((D_MODEL, W_IN_BLOCK), lambda i: (0, j), pipeline_mode=pl.Buffered(1))
    return pl.pallas_call(
        functools.partial(_ln_proj_kernel, tiles_per_seq=tiles_per_seq),
        grid=(T // tm,),
        in_specs=[
            pl.BlockSpec((tm, D_MODEL), lambda i: (i, 0)),
            _resident((1, D_MODEL)), _resident((1, D_MODEL)), w_blk(0), w_blk(1), w_blk(2),
            pl.BlockSpec((1, M, D_MODEL), lambda i: (i // tiles_per_seq, 0, 0)),
            _resident((D_MODEL, 2 * MEM_WIDTH)),
        ],
        out_specs=[
            pl.BlockSpec((tm, 3 * NA_WIDTH), lambda i: (i, 0)),
            head_spec, head_spec,
            pl.BlockSpec((tm, HG_WIDTH), lambda i: (i, 0)),
            head_spec, head_spec,
            pl.BlockSpec((tm, MEM_WIDTH), lambda i: (i, 0)),
        ],
        out_shape=[
            jax.ShapeDtypeStruct((T, 3 * NA_WIDTH), BF16),
            head_f32,
            jax.ShapeDtypeStruct((HG_HEADS, T, HG_DIM), BF16),
            jax.ShapeDtypeStruct((T, HG_WIDTH), F32),
            head_f32, head_f32,
            jax.ShapeDtypeStruct((T, MEM_WIDTH), BF16),
        ],
        scratch_shapes=[pltpu.VMEM((M, MEM_WIDTH), BF16), pltpu.VMEM((M, MEM_WIDTH), BF16)],
        compiler_params=pltpu.CompilerParams(
            dimension_semantics=("arbitrary",), vmem_limit_bytes=VMEM_LIMIT),
        name="ln_proj",
    )(x2, g, b, w_in_b, w_in_b, w_in_b, mem, wkv)


def _na_bias_bands(rpb):
    qc = np.arange(GRID_W)[None, :]
    kc = np.arange(GRID_W)[:, None]
    c0 = np.clip(qc - NA_KW // 2, 0, GRID_W - NA_KW)
    valid = (kc >= c0) & (kc < c0 + NA_KW)
    dc = kc - qc + NA_KW - 1
    onehot = ((dc[None] == np.arange(2 * NA_KW - 1)[:, None, None]) & valid[None]).astype(np.float32)
    bnd = jnp.einsum("hdj,jkq->hdkq", rpb.astype(F32), jnp.asarray(onehot), precision=lax.Precision.HIGHEST)
    bnd = bnd + jnp.asarray(np.where(valid, 0.0, MASK_NEG).astype(np.float32))
    pad = jnp.full((NA_HEADS, 4, GRID_W, GRID_W), MASK_NEG, F32)
    return jnp.concatenate([pad, bnd, pad], axis=1)


def _na_kernel(q_ref, k_ref, v_ref, band_ref, *rest, rows, n_cast):
    cast_in, o_ref, cast_out, tab_ref = rest[:n_cast], rest[n_cast], rest[n_cast + 1:-1], rest[-1]
    for src, dst in zip(cast_in, cast_out):
        dst[...] = src[...].astype(dst.dtype)

    @pl.when(pl.program_id(1) == 0)
    def _():
        masked = jnp.full((GRID_W, GRID_W), MASK_NEG, F32)
        cat = lambda a, b: jnp.concatenate([a, b], axis=-1)
        for h in range(2):
            for i in range(NA_PAIRS):
                left, right = band_ref[h, i + 1], band_ref[h, i]
                tab_ref[h, i] = cat(left, right)
                tab_ref[h, NA_PAIRS + i] = cat(left, masked)
                tab_ref[h, 2 * NA_PAIRS + i] = cat(masked, right)
            tab_ref[h, 3 * NA_PAIRS] = cat(masked, masked)

    scale = NA_HEAD_DIM ** -0.5
    assert np.log2(scale) == np.round(np.log2(scale))
    first = lax.broadcasted_iota(jnp.int32, (2 * GRID_W, LANES), 1) < NA_HEAD_DIM

    def window_start(g):
        return int(np.clip(g * NA_GROUP - NA_KH // 2, 0, rows - NA_WIN))

    def table_index(g, jp, a):
        r, kr = g * NA_GROUP + 2 * jp, window_start(g) + a
        in_window = lambda row: 0 <= kr - int(np.clip(row - NA_KH // 2, 0, rows - NA_KH)) < NA_KH
        lv, rv = in_window(r), in_window(r + 1)
        i = kr - r + (NA_KH - 1) + 3
        return i if lv and rv else NA_PAIRS + i if lv else 2 * NA_PAIRS + i if rv else 3 * NA_PAIRS

    def live_rows(g, jp):
        live = [a for a in range(NA_WIN) if table_index(g, jp, a) != 3 * NA_PAIRS]
        assert live == list(range(live[0], live[-1] + 1))
        return live[0], live[-1] + 1

    def key_rows(g, jp):
        a0, a1 = live_rows(g, jp)
        ws = window_start(g)
        return slice((ws + a0) * GRID_W, (ws + a1) * GRID_W)

    def query_rows(g, jp):
        r = g * NA_GROUP + 2 * jp
        return slice(r * GRID_W, (r + 2) * GRID_W)

    def scores(g, jp):
        q = q_ref[0, query_rows(g, jp), :] * jnp.asarray(scale, BF16)
        zero = jnp.zeros_like(q)
        qs = jnp.concatenate([jnp.where(first, q, zero), jnp.where(first, zero, q)], axis=0)
        return _dot_nt(k_ref[0, key_rows(g, jp), :], qs)

    def softmax(g, jp, s_t):
        a0, a1 = live_rows(g, jp)
        ps, inv_ls = [], []
        for h in range(2):
            bias = jnp.concatenate([tab_ref[h, table_index(g, jp, a)] for a in range(a0, a1)], axis=0)
            s = (s_t[:, h * LANES:(h + 1) * LANES] + bias).reshape(a1 - a0, GRID_W, LANES)
            m = jnp.max(jnp.max(s, axis=0), axis=0, keepdims=True)
            p = jnp.exp(s - m)
            inv_ls.append(1.0 / jnp.sum(jnp.sum(p, axis=0), axis=0, keepdims=True))
            ps.append(p.astype(BF16).reshape((a1 - a0) * GRID_W, LANES))
        return jnp.concatenate(ps, axis=1), jnp.concatenate(inv_ls, axis=1)

    def output(g, jp, p_t, inv_l):
        o_t = _dot_tn(v_ref[0, key_rows(g, jp), :], p_t) * inv_l
        o = jnp.concatenate([o_t[:NA_HEAD_DIM, :LANES], o_t[NA_HEAD_DIM:, LANES:]], axis=0).T
        o_ref[0, query_rows(g, jp), :] = o.astype(o_ref.dtype)

    pairs = range(NA_GROUP // 2)
    n_groups = rows // NA_GROUP
    s_next = [scores(0, jp) for jp in pairs]
    for g in range(n_groups):
        s_cur = s_next
        if g + 1 < n_groups:
            s_next = [scores(g + 1, jp) for jp in pairs]
        probs = [softmax(g, jp, s_cur[jp]) for jp in pairs]
        for jp in pairs:
            output(g, jp, *probs[jp])


def _na_attn(qkv3, bands, casts):
    B, S, _ = qkv3.shape
    rows = S // GRID_W
    assert rows % NA_GROUP == 0 and rows >= NA_WIN and bands.shape[1] == NA_PAIRS + 1
    n_pairs = NA_HEADS // 2
    steps = n_pairs * B
    blk = lambda c0: pl.BlockSpec((1, S, LANES), lambda hp, b: (b, 0, c0 + hp))
    cast_in, cast_out, cast_shape = [], [], []
    for w, c0, nc in casts:
        slab = w.shape[0] // steps
        assert slab * steps == w.shape[0] and slab % 16 == 0 and c0 % nc == 0
        cast_in.append(pl.BlockSpec((slab, nc), lambda hp, b, j=c0 // nc: (hp * B + b, j)))
        cast_out.append(pl.BlockSpec((slab, nc), lambda hp, b: (hp * B + b, 0)))
        cast_shape.append(jax.ShapeDtypeStruct((w.shape[0], nc), BF16))
    out = pl.pallas_call(
        functools.partial(_na_kernel, rows=rows, n_cast=len(casts)),
        grid=(n_pairs, B),
        in_specs=[
            blk(0), blk(n_pairs), blk(2 * n_pairs),
            pl.BlockSpec((2, NA_PAIRS + 1, GRID_W, GRID_W), lambda hp, b: (hp, 0, 0, 0)),
        ] + cast_in,
        out_specs=[pl.BlockSpec((1, S, LANES), lambda hp, b: (b, 0, hp))] + cast_out,
        out_shape=[jax.ShapeDtypeStruct((B, S, NA_WIDTH), BF16)] + cast_shape,
        scratch_shapes=[pltpu.VMEM((2, 3 * NA_PAIRS + 1, GRID_W, LANES), F32)],
        compiler_params=pltpu.CompilerParams(
            dimension_semantics=("arbitrary", "arbitrary"), vmem_limit_bytes=VMEM_LIMIT),
        name="na_attn",
    )(qkv3, qkv3, qkv3, bands, *[w for w, _, _ in casts])
    return out[0], out[1:]


def _hgrn_scales_step(row, col, n, reverse):
    C = row.shape[0]
    nv = C // SUBLANES
    if n < SUBLANES:
        r3 = row.reshape(nv, SUBLANES, LANES)
        c3 = col.reshape(nv, SUBLANES, LANES)
        sub = lax.broadcasted_iota(jnp.int32, (nv, SUBLANES, LANES), 1)
        in_g = (sub & n) != 0
        bc = lambda i: jnp.broadcast_to(r3[:, i:i + 1, :], r3.shape)
        if n == 1:
            up = pltpu.roll(r3, 1, axis=1)
            dn = pltpu.roll(r3, SUBLANES - 1, axis=1)
            if not reverse:
                new_r = jnp.where(in_g, r3 * up, r3)
                new_c = jnp.where(in_g, c3, c3 * dn)
            else:
                new_r = jnp.where(in_g, r3, r3 * dn)
                new_c = jnp.where(in_g, c3 * up, c3)
        else:
            lo = sub < 4
            if not reverse:
                if n == 2:
                    tf = jnp.where(lo, bc(1), bc(5))
                    tg = jnp.where(lo, bc(3), bc(7))
                else:
                    tf, tg = bc(3), bc(7)
            else:
                if n == 2:
                    tf = jnp.where(lo, bc(0), bc(4))
                    tg = jnp.where(lo, bc(2), bc(6))
                else:
                    tf, tg = bc(0), bc(4)
            if not reverse:
                new_r = jnp.where(in_g, r3 * tf, r3)
                new_c = jnp.where(in_g, c3, c3 * tg)
            else:
                new_r = jnp.where(in_g, r3, r3 * tg)
                new_c = jnp.where(in_g, c3 * tf, c3)
        return new_r.reshape(C, LANES), new_c.reshape(C, LANES)

    new_r, new_c = [], []
    for base in range(0, C, 2 * n):
        f_sl, g_sl = slice(base, base + n), slice(base + n, base + 2 * n)
        if not reverse:
            tf, tg = row[base + n - 1:base + n], row[base + 2 * n - 1:base + 2 * n]
            new_r += [row[f_sl], row[g_sl] * tf]
            new_c += [col[f_sl] * tg, col[g_sl]]
        else:
            tf, tg = row[base:base + 1], row[base + n:base + n + 1]
            new_r += [row[f_sl] * tg, row[g_sl]]
            new_c += [col[f_sl], col[g_sl] * tf]
    return jnp.concatenate(new_r, axis=0), jnp.concatenate(new_c, axis=0)


def _hgrn_direction(q, fpre, lb, v_bf, state_ref, level, reverse):
    C = q.shape[0]
    hc = C // 2
    halves = (slice(0, hc), slice(hc, C))
    f = lb + (1.0 - lb) * jax.nn.sigmoid(fpre)
    k = 1.0 - f

    def pick(lv, lhs, rhs, acc):
        return [jnp.where(level == lv, _dot(lhs[h].astype(BF16), rhs[h].T.astype(BF16)).astype(BF16), a)
                for h, a in zip(halves, acc)]

    a_diag = pick(HG_LEVELS, q, k, [jnp.zeros((hc, hc), BF16)] * 2)
    row, col = f, jnp.ones_like(f)
    n = 1
    for lv in range(HG_LEVELS):
        a_diag = pick(lv, q * row, k * col, a_diag)
        row, col = _hgrn_scales_step(row, col, n, reverse)
        n *= 2

    qs, ks = (q * row).astype(BF16), k * col
    lo, hi = halves
    if not reverse:
        a_x = _dot(qs[hi], ks[lo].T.astype(BF16)).astype(BF16)
        o_lo = _dot(a_diag[0], v_bf[lo])
        o_hi = _dot(jnp.concatenate([a_x, a_diag[1]], axis=1), v_bf)
    else:
        a_x = _dot(qs[lo], ks[hi].T.astype(BF16)).astype(BF16)
        o_lo = _dot(jnp.concatenate([a_diag[0], a_x], axis=1), v_bf)
        o_hi = _dot(a_diag[1], v_bf[hi])
    row, col = _hgrn_scales_step(row, col, n, reverse)

    st = state_ref[...]
    o = jnp.concatenate([o_lo, o_hi], axis=0) + _dot_nt((q * row).astype(BF16), st.astype(BF16))
    total = row[C - 1:C, :] if not reverse else row[0:1, :]
    state_ref[...] = st * total + _dot_tn(v_bf, (k * col).astype(BF16))
    return o


def _hgrn_kernel(qf_ref, vf_ref, ff_ref, qb_ref, vb_ref, fb_ref, lbf_ref, lbb_ref, lvf_ref, lvb_ref,
                 of_ref, ob_ref, sf_ref, sb_ref):
    @pl.when(pl.program_id(2) == 0)
    def _():
        sf_ref[...] = jnp.zeros_like(sf_ref)
        sb_ref[...] = jnp.zeros_like(sb_ref)

    for h in range(HG_HPS):
        qf = jax.nn.silu(qf_ref[h])
        of_ref[h] = _hgrn_direction(qf, ff_ref[h], lbf_ref[h], vf_ref[h], sf_ref.at[h], lvf_ref[...], False)
        qb = jax.nn.silu(qb_ref[h])
        ob_ref[h] = _hgrn_direction(qb, fb_ref[h], lbb_ref[h], vb_ref[h], sb_ref.at[h], lvb_ref[...], True)


def _hgrn_level_tables():
    hc = HG_CHUNK // 2
    t = np.arange(hc)[:, None]
    s = np.arange(hc)[None, :]
    x = t ^ s
    hb = np.where(x > 0, np.floor(np.log2(np.maximum(x, 1))), HG_LEVELS).astype(np.float32)
    return jnp.asarray(np.where(t >= s, hb, -1.0), BF16), jnp.asarray(np.where(t <= s, hb, -1.0), BF16)


def _hgrn(hq, hi, hff, hfb, lbf, lbb, batch):
    H, T, _ = hq.shape
    C = HG_CHUNK
    nc = T // batch // C
    fwd = pl.BlockSpec((HG_HPS, C, HG_DIM), lambda b, h, c: (h, b * nc + c, 0))
    bwd = pl.BlockSpec((HG_HPS, C, HG_DIM), lambda b, h, c: (h, b * nc + nc - 1 - c, 0))
    lb_spec = pl.BlockSpec((HG_HPS, 1, HG_DIM), lambda b, h, c: (h, 0, 0))
    lv_spec = _resident((C // 2, C // 2))
    lvf, lvb = _hgrn_level_tables()
    out = jax.ShapeDtypeStruct((H, T, HG_DIM), F32)
    state = pltpu.VMEM((HG_HPS, HG_DIM, HG_DIM), F32)
    return pl.pallas_call(
        _hgrn_kernel,
        grid=(batch, H // HG_HPS, nc),
        in_specs=[fwd, fwd, fwd, bwd, bwd, bwd, lb_spec, lb_spec, lv_spec, lv_spec],
        out_specs=[fwd, bwd],
        out_shape=[out, out],
        scratch_shapes=[state, state],
        compiler_params=pltpu.CompilerParams(
            dimension_semantics=("parallel", "parallel", "arbitrary"), vmem_limit_bytes=VMEM_LIMIT),
        name="hgrn",
    )(hq, hi, hff, hq, hi, hfb, lbf, lbb, lvf, lvb)


def _merge_kernel(x_ref, ge_ref, be_ref, yna_ref, of_ref, ob_ref, hog_ref, ng_ref, ymem_ref,
                  wg0_ref, wg1_ref, wna_ref, whg_ref, wmem_ref, wout_ref, g1_ref, b1_ref, o_ref):
    wg_refs = (wg0_ref, wg1_ref)

    def sub_tile(rows):
        xn = _layer_norm(x_ref[rows, :], ge_ref[...], be_ref[...])
        xb = xn.astype(BF16)
        yield
        pre = jnp.concatenate([_dot(xb, w[...]) for w in wg_refs], axis=-1)
        yield
        heads = []
        for h in range(HG_HEADS):
            o = of_ref[h, rows, :] + ob_ref[h, rows, :]
            heads.append(o * lax.rsqrt(jnp.mean(o * o, axis=-1, keepdims=True) + RMS_EPS))
        y_hg = jnp.concatenate(heads, axis=-1) * ng_ref[...] * jax.nn.silu(hog_ref[rows, :])
        yield
        branches = (_dot(yna_ref[rows, :], wna_ref[...]), _dot(y_hg.astype(BF16), whg_ref[...]),
                    _dot(ymem_ref[rows, :], wmem_ref[...]))
        yield
        gated = [jax.nn.sigmoid(pre[:, j * D_MODEL:(j + 1) * D_MODEL]) * br for j, br in enumerate(branches)]
        merged = gated[0] + gated[1] + gated[2]
        yield
        y = ALPHA * xn + _dot(merged.astype(BF16), wout_ref[...])
        yield
        o_ref[rows, :] = _layer_norm(y, g1_ref[...], b1_ref[...])

    tiles = _sub_tiles(x_ref.shape[0])
    for i in range(0, len(tiles), 2):
        _interleave([sub_tile(rows) for rows in tiles[i:i + 2]])


def _merge(x2, ge, be, y_na, o_f, o_b, hog, ng, y_mem, wg0, wg1, wna, whg, wmem, wout, g1, b1):
    T = x2.shape[0]
    tm = TOKEN_TILE
    row = lambda n: pl.BlockSpec((tm, n), lambda i: (i, 0))
    vec = lambda n: _resident((1, n))
    head = pl.BlockSpec((HG_HEADS, tm, HG_DIM), lambda i: (0, i, 0))
    assert wg0.shape[1] + wg1.shape[1] == 3 * D_MODEL
    return pl.pallas_call(
        _merge_kernel,
        grid=(T // tm,),
        in_specs=[
            row(D_MODEL), vec(D_MODEL), vec(D_MODEL),
            row(NA_WIDTH), head, head, row(HG_WIDTH), vec(HG_WIDTH), row(MEM_WIDTH),
            _resident(wg0.shape), _resident(wg1.shape), _resident((NA_WIDTH, D_MODEL)),
            _resident((HG_WIDTH, D_MODEL)),
            _resident((MEM_WIDTH, D_MODEL)), _resident((D_MODEL, D_MODEL)), vec(D_MODEL), vec(D_MODEL),
        ],
        out_specs=row(D_MODEL),
        out_shape=jax.ShapeDtypeStruct((T, D_MODEL), F32),
        compiler_params=pltpu.CompilerParams(
            dimension_semantics=("parallel",), vmem_limit_bytes=VMEM_LIMIT),
        name="merge",
    )(x2, ge, be, y_na, o_f, o_b, hog, ng, y_mem, wg0, wg1, wna, whg, wmem, wout, g1, b1)


def _ffn_kernel(x_ref, w1_ref, w2_ref, g_ref, b_ref, o_ref):
    def sub_tile(rows):
        x = x_ref[rows, :]
        xb = x.astype(BF16)
        acc = ALPHA * x
        for j in range(D_FF // D_MODEL):
            sl = slice(j * D_MODEL, (j + 1) * D_MODEL)
            h = jnp.maximum(_dot(xb, w1_ref[:, sl]), 0.0)
            yield
            acc = acc + _dot((h * h).astype(BF16), w2_ref[sl, :])
            yield
        o_ref[rows, :] = _layer_norm(acc, g_ref[...], b_ref[...])

    _interleave([sub_tile(rows) for rows in _sub_tiles(x_ref.shape[0])])


def _ffn(x1, w1, w2, g, b):
    T = x1.shape[0]
    tm = TOKEN_TILE
    return pl.pallas_call(
        _ffn_kernel,
        grid=(T // tm,),
        in_specs=[
            pl.BlockSpec((tm, D_MODEL), lambda i: (i, 0)),
            _resident((D_MODEL, D_FF)), _resident((D_FF, D_MODEL)),
            _resident((1, D_MODEL)), _resident((1, D_MODEL)),
        ],
        out_specs=pl.BlockSpec((tm, D_MODEL), lambda i: (i, 0)),
        out_shape=jax.ShapeDtypeStruct((T, D_MODEL), F32),
        compiler_params=pltpu.CompilerParams(
            dimension_semantics=("parallel",), vmem_limit_bytes=VMEM_LIMIT),
        name="ffn",
    )(x1, w1, w2, g, b)


def kernel(x, mem, ln_emb_g, ln_emb_b, w_in, na_rpb, hg_lb_logits, hg_norm_g, w_mem_kv, w_branch_na,
           w_branch_hg, w_branch_mem, w_out, ln1_g, ln1_b, w_ff1, w_ff2, ln2_g, ln2_b):
    B, S, D = x.shape
    assert D == D_MODEL and S % HG_CHUNK == 0 and S % GRID_W == 0 and S // GRID_W >= NA_KH
    assert w_in.shape[0] == DEPTH and (B * S) % TOKEN_TILE == 0
    T = B * S
    l = 0
    vec = lambda a: a.reshape(1, -1).astype(F32)

    w_br = w_in[l, :, :BRANCH_COLS].astype(BF16)
    lb_all = jnp.cumsum(jax.nn.softmax(hg_lb_logits.astype(F32), axis=1), axis=1)
    lbf = lb_all[0, l].reshape(HG_HEADS, 1, HG_DIM)
    lbb = lb_all[1, l].reshape(HG_HEADS, 1, HG_DIM)
    bands = _na_bias_bands(na_rpb[l])

    x2 = x.reshape(T, D)
    ge, be = vec(ln_emb_g), vec(ln_emb_b)
    qkv, hq, hi, hog, hff, hfb, y_mem = _ln_proj(x2, ge, be, w_br, mem, w_mem_kv[l].astype(BF16), S)
    casts = [(w_in[l], BRANCH_COLS, W_IN_BLOCK), (w_in[l], BRANCH_COLS + W_IN_BLOCK, W_IN_BLOCK),
             (w_branch_na[l], 0, D_MODEL), (w_branch_hg[l], 0, D_MODEL), (w_branch_mem[l], 0, D_MODEL),
             (w_out[l], 0, D_MODEL), (w_ff1[l], 0, D_FF), (w_ff2[l], 0, D_MODEL)]
    y_na, (wg0, wg1, wna, whg, wmem, wout, w1, w2) = _na_attn(qkv.reshape(B, S, 3 * NA_WIDTH), bands, casts)
    o_f, o_b = _hgrn(hq, hi, hff, hfb, lbf, lbb, B)
    x1 = _merge(x2, ge, be, y_na.reshape(T, NA_WIDTH), o_f, o_b, hog, vec(hg_norm_g[l]), y_mem,
                wg0, wg1, wna, whg, wmem, wout, vec(ln1_g[l]), vec(ln1_b[l]))
    out = _ffn(x1, w1, w2, vec(ln2_g[l]), vec(ln2_b[l]))
    return out.reshape(B, S, D)
```
